```python
import math
import jax, jax.numpy as jnp
from jax import lax
import numpy as np

D_MODEL = 1024
BATCH = 8
SEQ = 2048
DEPTH = 2

CHUNK = 64
D_MIX = D_MODEL
GLA_HEADS = 4
GLA_WIDTH = D_MIX // 2
GLA_DV = GLA_WIDTH // GLA_HEADS
GLA_DK = GLA_DV // 2
GLA_GATE_RANK = 16
GLA_GATE_NORMALIZER = 16.0
SGU_WIDTH = D_MIX - GLA_WIDTH
SGU_GROUPS = 4
SGU_GROUP_DIM = SGU_WIDTH // SGU_GROUPS
SGU_WINDOW = 128
COL_Q = GLA_HEADS * GLA_DK
COL_K = GLA_HEADS * GLA_DK
COL_V = GLA_WIDTH
COL_R = GLA_WIDTH
COL_A = GLA_GATE_RANK
COL_SU = SGU_WIDTH
COL_SV = SGU_WIDTH
IN_COLS = COL_Q + COL_K + COL_V + COL_R + COL_A + COL_SU + COL_SV
PEER_HEADS = 8
PEER_NKEYS = 128
PEER_EXPERTS = PEER_NKEYS * PEER_NKEYS
PEER_TOPK = 16
PEER_DQ = 256
PEER_DQ_HALF = PEER_DQ // 2
PEER_HK = PEER_HEADS * PEER_TOPK
PEER_TOKEN_BLOCK = 128
LN_EPS = 1e-5
ALPHA = (2.0 * DEPTH) ** 0.25
BETA = (8.0 * DEPTH) ** -0.25

kernel_name = "hybrid_gla_gmlp_peer_deepnorm"


def _layer_norm(x, g, b):
    xf = x.astype(jnp.float32)
    mu = jnp.mean(xf, axis=-1, keepdims=True)
    var = jnp.mean(jnp.square(xf - mu), axis=-1, keepdims=True)
    y = (xf - mu) * lax.rsqrt(var + LN_EPS)
    return (y * g.astype(jnp.float32) + b.astype(jnp.float32)).astype(x.dtype)


def _rms_norm(x, g):
    xf = x.astype(jnp.float32)
    y = xf * lax.rsqrt(jnp.mean(jnp.square(xf), axis=-1, keepdims=True) + LN_EPS)
    return (y * g.astype(jnp.float32)).astype(x.dtype)


def _gla_chunked(q, k, v, logg):
    B, S, H, DK = q.shape
    DV = v.shape[-1]
    N = S // CHUNK

    def to_chunks(t):
        return t.astype(jnp.float32).reshape(B, N, CHUNK, H, t.shape[-1]).transpose(1, 0, 3, 2, 4)

    qc, kc, vc, gc = to_chunks(q), to_chunks(k), to_chunks(v), to_chunks(logg)
    causal = jnp.tril(jnp.ones((CHUNK, CHUNK), dtype=bool))

    def step(state, inp):
        qb, kb, vb, gb = inp
        bcum = jnp.cumsum(gb, axis=2)
        diff = bcum[:, :, :, None, :] - bcum[:, :, None, :, :]
        decay = jnp.exp(jnp.where(causal[:, :, None], diff, -jnp.inf))
        attn = jnp.einsum('bhtk,bhsk,bhtsk->bhts', qb, kb, decay)
        o = jnp.einsum('bhts,bhsv->bhtv', attn, vb) + \
            jnp.einsum('bhtk,bhkv->bhtv', qb * jnp.exp(bcum), state)
        b_last = bcum[:, :, -1:, :]
        state = jnp.exp(b_last[:, :, 0, :])[..., None] * state + \
            jnp.einsum('bhsk,bhsv->bhkv', kb * jnp.exp(b_last - bcum), vb)
        return state, o

    s0 = jnp.zeros((B, H, DK, DV), jnp.float32)
    _, o = lax.scan(step, s0, (qc, kc, vc, gc))
    return o.transpose(1, 0, 3, 2, 4).reshape(B, S, H, DV).astype(v.dtype)


def _spatial_gating(u_pre, v_pre, ln_g, ln_b, w_s, b_s):
    B, S, _ = u_pre.shape
    u = jax.nn.gelu(u_pre, approximate=False)
    v = _layer_norm(jax.nn.gelu(v_pre, approximate=False), ln_g, ln_b)
    blk = jnp.arange(SGU_WINDOW) // CHUNK
    mask = blk[None, :] <= blk[:, None]
    w = jnp.where(mask[None], w_s, 0)
    vb = v.reshape(B, S // SGU_WINDOW, SGU_WINDOW, SGU_GROUPS, SGU_GROUP_DIM)
    sv = jnp.einsum('gts,bnsgc->bntgc', w, vb) + b_s.T[:, :, None]
    return u * sv.reshape(B, S, SGU_WIDTH)


def _peer(x, wq, k1, k2, u_tab, v_tab):
    B, S, D = x.shape
    T = B * S
    xt = x.reshape(T, D)
    q = (xt @ wq).reshape(T, PEER_HEADS, 2, PEER_DQ_HALF)
    s1 = jnp.einsum('thd,nd->thn', q[:, :, 0], k1)
    s2 = jnp.einsum('thd,nd->thn', q[:, :, 1], k2)
    v1, i1 = lax.top_k(s1, PEER_TOPK)
    v2, i2 = lax.top_k(s2, PEER_TOPK)
    cand = (v1[..., :, None] + v2[..., None, :]).reshape(T, PEER_HEADS, PEER_TOPK * PEER_TOPK)
    cidx = (i1[..., :, None] * PEER_NKEYS + i2[..., None, :]).reshape(T, PEER_HEADS, PEER_TOPK * PEER_TOPK)
    sc, pos = lax.top_k(cand, PEER_TOPK)
    eidx = jnp.take_along_axis(cidx, pos, axis=-1).reshape(T, PEER_HK)
    gate = jax.nn.softmax(sc.astype(jnp.float32), axis=-1).astype(x.dtype).reshape(T, PEER_HK)
    nb = T // PEER_TOKEN_BLOCK

    def expert_block(args):
        xb, ib, gb = args
        ub = jnp.take(u_tab, ib, axis=0)
        act = jax.nn.gelu(jnp.einsum('td,tkd->tk', xb, ub), approximate=False)
        vb = jnp.take(v_tab, ib, axis=0)
        return jnp.einsum('tk,tkd->td', gb * act, vb)

    out = lax.map(expert_block, (xt.reshape(nb, PEER_TOKEN_BLOCK, D),
                                 eidx.reshape(nb, PEER_TOKEN_BLOCK, PEER_HK),
                                 gate.reshape(nb, PEER_TOKEN_BLOCK, PEER_HK)))
    return out.reshape(B, S, D)


def _mixer(x, w_in, w_gate_up, b_gate, gla_norm_g, sgu_ln_g, sgu_ln_b, sgu_w, sgu_b, w_out):
    B, S, _ = x.shape
    h = x @ w_in
    splits = np.cumsum([COL_Q, COL_K, COL_V, COL_R, COL_A, COL_SU]).tolist()
    q, k, v, r, a, su, sv = jnp.split(h, splits, axis=-1)
    logg = jax.nn.log_sigmoid((a @ w_gate_up + b_gate).astype(jnp.float32)) / GLA_GATE_NORMALIZER
    q = q.reshape(B, S, GLA_HEADS, GLA_DK) * (GLA_DK ** -0.5)
    k = k.reshape(B, S, GLA_HEADS, GLA_DK)
    v = v.reshape(B, S, GLA_HEADS, GLA_DV)
    o = _gla_chunked(q, k, v, logg.reshape(B, S, GLA_HEADS, GLA_DK))
    o = _rms_norm(o, gla_norm_g).reshape(B, S, GLA_WIDTH) * jax.nn.silu(r)
    g = _spatial_gating(su, sv, sgu_ln_g, sgu_ln_b, sgu_w, sgu_b)
    return jnp.concatenate([o, g], axis=-1) @ w_out


def setup_inputs(seed: int = 0) -> dict:
    key = jax.random.key(seed)
    ks = jax.random.split(key, 24)
    L, D = DEPTH, D_MODEL
    nrm = lambda k, shape, s: jax.random.normal(k, shape, jnp.float32) * s
    return {
        "x": nrm(ks[0], (BATCH, SEQ, D), 1.0),
        "ln_in_g": 1.0 + nrm(ks[1], (D,), 0.02),
        "ln_in_b": nrm(ks[2], (D,), 0.02),
        "w_in": nrm(ks[3], (L, D, IN_COLS), D ** -0.5),
        "w_gate_up": nrm(ks[4], (L, GLA_GATE_RANK, GLA_HEADS * GLA_DK), GLA_GATE_RANK ** -0.5),
        "b_gate": nrm(ks[5], (L, GLA_HEADS * GLA_DK), 0.1),
        "gla_norm_g": 1.0 + nrm(ks[6], (L, GLA_DV), 0.02),
        "sgu_ln_g": 1.0 + nrm(ks[7], (L, SGU_WIDTH), 0.02),
        "sgu_ln_b": nrm(ks[8], (L, SGU_WIDTH), 0.02),
        "sgu_w": nrm(ks[9], (L, SGU_GROUPS, SGU_WINDOW, SGU_WINDOW), SGU_WINDOW ** -0.5),
        "sgu_b": 1.0 + nrm(ks[10], (L, SGU_GROUPS, SGU_WINDOW), 0.02),
        "w_out": nrm(ks[11], (L, D_MIX, D), BETA * D_MIX ** -0.5),
        "ln1_g": 1.0 + nrm(ks[12], (L, D), 0.02),
        "ln1_b": nrm(ks[13], (L, D), 0.02),
        "peer_wq": nrm(ks[14], (L, D, PEER_HEADS * PEER_DQ), D ** -0.5),
        "peer_k1": nrm(ks[15], (L, PEER_NKEYS, PEER_DQ_HALF), PEER_DQ_HALF ** -0.5),
        "peer_k2": nrm(ks[16], (L, PEER_NKEYS, PEER_DQ_HALF), PEER_DQ_HALF ** -0.5),
        "peer_u": nrm(ks[17], (L, PEER_EXPERTS, D), D ** -0.5),
        "peer_v": nrm(ks[18], (L, PEER_EXPERTS, D), BETA * PEER_HEADS ** -0.5),
        "ln2_g": 1.0 + nrm(ks[19], (L, D), 0.02),
        "ln2_b": nrm(ks[20], (L, D), 0.02),
    }


def reference(x, ln_in_g, ln_in_b, w_in, w_gate_up, b_gate, gla_norm_g, sgu_ln_g, sgu_ln_b,
              sgu_w, sgu_b, w_out, ln1_g, ln1_b, peer_wq, peer_k1, peer_k2, peer_u, peer_v,
              ln2_g, ln2_b):
    h = _layer_norm(x, ln_in_g, ln_in_b)
    for l in range(DEPTH):
        mix = _mixer(h, w_in[l], w_gate_up[l], b_gate[l], gla_norm_g[l], sgu_ln_g[l], sgu_ln_b[l],
                     sgu_w[l], sgu_b[l], w_out[l])
        h = _layer_norm(ALPHA * h + mix, ln1_g[l], ln1_b[l])
        ffn = _peer(h, peer_wq[l], peer_k1[l], peer_k2[l], peer_u[l], peer_v[l])
        h = _layer_norm(ALPHA * h + ffn, ln2_g[l], ln2_b[l])
    return h
```

```python
import functools
import math

import jax
import jax.numpy as jnp
from jax import lax
from jax.experimental import pallas as pl
from jax.experimental.pallas import tpu as pltpu

F32 = jnp.float32
BF16 = jnp.bfloat16

D_MODEL = 1024
DEPTH = 2
CHUNK = 64
GLA_HEADS = 4
GLA_DK = 64
GLA_DV = 128
GLA_WIDTH = GLA_HEADS * GLA_DV
GLA_QK = GLA_HEADS * GLA_DK
GLA_GATE_RANK = 16
GLA_GATE_NORMALIZER = 16.0
SGU_WIDTH = 512
SGU_GROUPS = 4
SGU_GROUP_DIM = 128
SGU_WINDOW = 128
PEER_HEADS = 8
PEER_NKEYS = 128
PEER_EXPERTS = PEER_NKEYS * PEER_NKEYS
PEER_TOPK = 16
PEER_DQ = 256
PEER_DQ_HALF = 128
LN_EPS = 1e-5
ALPHA = (2.0 * DEPTH) ** 0.25

LANES = 128
SUBLANES = 8

COL_Q = 0
COL_K = COL_Q + GLA_QK
COL_V = COL_K + GLA_QK
COL_R = COL_V + GLA_WIDTH
COL_SU = COL_R + GLA_WIDTH
COL_SV = COL_SU + SGU_WIDTH
COL_A = COL_SV + SGU_WIDTH
A_PAD = LANES
P_COLS = COL_A + A_PAD

SUB = 16
EXP_CLAMP = 80.0
NEG_INF = float("-inf")

VMEM_LIMIT = 56 * 1024 * 1024


def _params(sem):
    return pltpu.CompilerParams(dimension_semantics=sem, vmem_limit_bytes=VMEM_LIMIT)


def _layer_norm(x, g, b):
    mu = jnp.mean(x, axis=-1, keepdims=True)
    xc = x - mu
    var = jnp.mean(xc * xc, axis=-1, keepdims=True)
    return xc * lax.rsqrt(var + LN_EPS) * g + b


def _gelu(x):
    return 0.5 * x * (1.0 + lax.erf(x * (1.0 / math.sqrt(2.0))))


def _dot(a, b):
    return jnp.dot(a, b, preferred_element_type=F32)


def _dot_nt(a, b):
    return lax.dot_general(a, b, (((1,), (1,)), ((), ())), preferred_element_type=F32)


def _dot_tn(a, b):
    return lax.dot_general(a, b, (((0,), (0,)), ((), ())), preferred_element_type=F32)


def _ln_kernel(x_ref, g_ref, b_ref, o_ref):
    o_ref[...] = _layer_norm(x_ref[...], g_ref[...], b_ref[...])


def _ln(x, g, b, tm):
    t, d = x.shape
    return pl.pallas_call(
        _ln_kernel,
        grid=(t // tm,),
        in_specs=[pl.BlockSpec((tm, d), lambda i: (i, 0)),
                  pl.BlockSpec((1, d), lambda i: (0, 0)),
                  pl.BlockSpec((1, d), lambda i: (0, 0))],
        out_specs=pl.BlockSpec((tm, d), lambda i: (i, 0)),
        out_shape=jax.ShapeDtypeStruct((t, d), F32),
        compiler_params=_params(("parallel",)),
        name="ln_in",
    )(x, g.reshape(1, d), b.reshape(1, d))


def _in_proj_kernel(h_ref, w_ref, o_ref):
    o_ref[...] = _dot(h_ref[...].astype(BF16), w_ref[...])


def _in_proj(h, w, tm):
    t, d = h.shape
    n = w.shape[1]
    return pl.pallas_call(
        _in_proj_kernel,
        grid=(t // tm,),
        in_specs=[pl.BlockSpec((tm, d), lambda i: (i, 0)),
                  pl.BlockSpec((d, n), lambda i: (0, 0))],
        out_specs=pl.BlockSpec((tm, n), lambda i: (i, 0)),
        out_shape=jax.ShapeDtypeStruct((t, n), F32),
        compiler_params=_params(("parallel",)),
        name="in_proj",
    )(h, w)


def _gla_kernel(q_ref, k_ref, v_ref, r_ref, a_ref, wg_ref, bg_ref, ng_ref, o_ref, st_ref,
                *, chunks):
    @pl.when(pl.program_id(1) == 0)
    def _():
        st_ref[...] = jnp.zeros_like(st_ref)

    z = jnp.dot(a_ref[...], wg_ref[...], preferred_element_type=F32,
                precision=lax.Precision.HIGHEST) + bg_ref[...]
    logg = (jnp.minimum(z, 0.0) - jnp.log1p(jnp.exp(-jnp.abs(z)))) * (1.0 / GLA_GATE_NORMALIZER)

    row = lax.broadcasted_iota(jnp.int32, (CHUNK, CHUNK), 0)
    col = lax.broadcasted_iota(jnp.int32, (CHUNK, CHUNK), 1)
    causal = col <= row
    tril = causal.astype(F32)
    lane_head = lax.broadcasted_iota(jnp.int32, (1, LANES), 1) // GLA_DK
    ng = ng_ref[...]

    for c in range(chunks):
        rows = slice(c * CHUNK, (c + 1) * CHUNK)
        bc_all = jnp.dot(tril, logg[rows, :], preferred_element_type=F32,
                         precision=lax.Precision.HIGHEST)
        for grp in range(GLA_QK // LANES):
            ls = slice(grp * LANES, (grp + 1) * LANES)
            qg = q_ref[rows, ls] * (GLA_DK ** -0.5)
            kg = k_ref[rows, ls]
            bg = bc_all[:, ls]
            b_last = bg[CHUNK - 1:CHUNK, :]
            q_inter = (qg * jnp.exp(bg)).astype(BF16)
            kd = kg * jnp.exp(b_last - bg)
            decay = jnp.exp(b_last)
            att = [[], []]
            for i in range(CHUNK // SUB):
                lo, hi = i * SUB, (i + 1) * SUB
                ref = bg[lo:lo + 1, :]
                qi = qg[lo:hi, :] * jnp.exp(bg[lo:hi, :] - ref)
                ki = (kg * jnp.exp(jnp.minimum(ref - bg, EXP_CLAMP))).astype(BF16)
                q2 = jnp.concatenate([jnp.where(lane_head == 0, qi, 0.0),
                                      jnp.where(lane_head == 1, qi, 0.0)], axis=0).astype(BF16)
                s = _dot_nt(q2, ki)
                att[0].append(s[:SUB, :])
                att[1].append(s[SUB:, :])
            for sub in range(LANES // GLA_DK):
                h = grp * (LANES // GLA_DK) + sub
                vs = slice(h * GLA_DV, (h + 1) * GLA_DV)
                vh_b = v_ref[rows, vs].astype(BF16)
                a_h = jnp.where(causal, jnp.concatenate(att[sub], axis=0), 0.0).astype(BF16)
                state_t = st_ref[h]
                o = _dot_nt(q_inter, state_t.astype(BF16)) + _dot(a_h, vh_b)
                kd_h = jnp.where(lane_head == sub, kd, 0.0).astype(BF16)
                st_ref[h] = state_t * decay + _dot_tn(vh_b, kd_h)
                ms = jnp.mean(o * o, axis=-1, keepdims=True)
                on = o * lax.rsqrt(ms + LN_EPS) * ng
                rh = r_ref[rows, vs]
                o_ref[rows, vs] = on * (rh * (1.0 / (1.0 + jnp.exp(-rh))))


def _gla(p, wg, bg, ng, batch, seq, ts):
    kern = functools.partial(_gla_kernel, chunks=ts // CHUNK)
    return pl.pallas_call(
        kern,
        grid=(batch, seq // ts),
        in_specs=[
            pl.BlockSpec((None, ts, GLA_QK), lambda b, s: (b, s, COL_Q // GLA_QK)),
            pl.BlockSpec((None, ts, GLA_QK), lambda b, s: (b, s, COL_K // GLA_QK)),
            pl.BlockSpec((None, ts, GLA_WIDTH), lambda b, s: (b, s, COL_V // GLA_WIDTH)),
            pl.BlockSpec((None, ts, GLA_WIDTH), lambda b, s: (b, s, COL_R // GLA_WIDTH)),
            pl.BlockSpec((None, ts, A_PAD), lambda b, s: (b, s, COL_A // A_PAD)),
            pl.BlockSpec((A_PAD, GLA_QK), lambda b, s: (0, 0)),
            pl.BlockSpec((1, GLA_QK), lambda b, s: (0, 0)),
            pl.BlockSpec((1, GLA_DV), lambda b, s: (0, 0)),
        ],
        out_specs=pl.BlockSpec((None, ts, GLA_WIDTH), lambda b, s: (b, s, 0)),
        out_shape=jax.ShapeDtypeStruct((batch, seq, GLA_WIDTH), F32),
        scratch_shapes=[pltpu.VMEM((GLA_HEADS, GLA_DV, LANES), F32)],
        compiler_params=_params(("parallel", "arbitrary")),
        name="gla",
    )(p, p, p, p, p, wg, bg, ng)


def _mix_out_kernel(su_ref, sv_ref, o_ref, h_ref, lg_ref, lb_ref, ws_ref, bs_ref, wo_ref,
                    g1_ref, b1_ref, out_ref, gate_ref, *, windows):
    u = _gelu(su_ref[...])
    v = _layer_norm(_gelu(sv_ref[...]), lg_ref[...], lb_ref[...])
    blk_r = lax.broadcasted_iota(jnp.int32, (SGU_WINDOW, SGU_WINDOW), 0) // CHUNK
    blk_c = lax.broadcasted_iota(jnp.int32, (SGU_WINDOW, SGU_WINDOW), 1) // CHUNK
    keep = blk_c <= blk_r
    for g in range(SGU_GROUPS):
        cs = slice(g * SGU_GROUP_DIM, (g + 1) * SGU_GROUP_DIM)
        w = jnp.where(keep, ws_ref[g], 0.0).astype(BF16)
        bias = bs_ref[g]
        for n in range(windows):
            rs = slice(n * SGU_WINDOW, (n + 1) * SGU_WINDOW)
            sv = _dot(w, v[rs, cs].astype(BF16)) + bias
            gate_ref[rs, cs] = u[rs, cs] * sv
    mix = _dot(o_ref[...].astype(BF16), wo_ref[0:GLA_WIDTH, :])
    mix = mix + _dot(gate_ref[...].astype(BF16), wo_ref[GLA_WIDTH:, :])
    out_ref[...] = _layer_norm(ALPHA * h_ref[...] + mix, g1_ref[...], b1_ref[...])


def _mix_out(p, o, h, lg, lb, ws, bs_full, wo, g1, b1, tm):
    t, d = h.shape
    kern = functools.partial(_mix_out_kernel, windows=tm // SGU_WINDOW)
    const2 = lambda i: (0, 0)
    const3 = lambda i: (0, 0, 0)
    return pl.pallas_call(
        kern,
        grid=(t // tm,),
        in_specs=[
            pl.BlockSpec((tm, SGU_WIDTH), lambda i: (i, COL_SU // SGU_WIDTH)),
            pl.BlockSpec((tm, SGU_WIDTH), lambda i: (i, COL_SV // SGU_WIDTH)),
            pl.BlockSpec((tm, GLA_WIDTH), lambda i: (i, 0)),
            pl.BlockSpec((tm, d), lambda i: (i, 0)),
            pl.BlockSpec((1, SGU_WIDTH), const2),
            pl.BlockSpec((1, SGU_WIDTH), const2),
            pl.BlockSpec((SGU_GROUPS, SGU_WINDOW, SGU_WINDOW), const3),
            pl.BlockSpec((SGU_GROUPS, SGU_WINDOW, SGU_GROUP_DIM), const3),
            pl.BlockSpec((d, d), const2),
            pl.BlockSpec((1, d), const2),
            pl.BlockSpec((1, d), const2),
        ],
        out_specs=pl.BlockSpec((tm, d), lambda i: (i, 0)),
        out_shape=jax.ShapeDtypeStruct((t, d), F32),
        scratch_shapes=[pltpu.VMEM((tm, SGU_WIDTH), F32)],
        compiler_params=_params(("parallel",)),
        name="mix_out",
    )(p, p, o, h, lg, lb, ws, bs_full, wo, g1, b1)


def _top_values(arr, vals_ref, want_rank):
    rank = jnp.full(arr.shape, float(PEER_TOPK), F32) if want_rank else None
    for r in range(PEER_TOPK):
        m = jnp.max(arr, axis=0, keepdims=True)
        hit = arr == m
        if want_rank:
            rank = jnp.where(hit, float(r), rank)
        arr = jnp.where(hit, NEG_INF, arr)
        vals_ref[r:r + 1, :] = m
    return rank


def _route_kernel(h_ref, wq_ref, k1_ref, k2_ref, xt_ref, rank2_ref, p2_ref, n1_ref, p1_ref,
                  a_ref, b_ref, c_ref):
    @pl.when(pl.program_id(1) == 0)
    def _():
        xt_ref[...] = jnp.transpose(h_ref[...]).astype(BF16)

    tm = h_ref.shape[0]
    q = _dot(h_ref[...].astype(BF16), wq_ref[...])
    s1 = _dot_nt(k1_ref[...], q[:, :PEER_DQ_HALF].astype(BF16))
    s2 = _dot_nt(k2_ref[...], q[:, PEER_DQ_HALF:].astype(BF16))

    _top_values(s1, a_ref, False)
    rank2 = _top_values(s2, b_ref, True)
    a = a_ref[...]
    b = b_ref[...]

    sub = lax.broadcasted_iota(jnp.int32, (SUBLANES, tm), 0)
    blocks = [a[0:1, :] + b, a[1:2, :] + b[0:SUBLANES, :]]
    for i in range(2, SUBLANES):
        keep = sub < (PEER_TOPK // (i + 1))
        blocks.append(jnp.where(keep, a[i:i + 1, :] + b[0:SUBLANES, :], NEG_INF))
    blocks.append(a[SUBLANES:, :] + b[0:1, :])
    cand = jnp.concatenate(blocks, axis=0)
    _top_values(cand, c_ref, False)
    c = c_ref[...]
    cmax = c[0:1, :]
    thr = c[PEER_TOPK - 1:PEER_TOPK, :]
    inv_z = 1.0 / jnp.sum(jnp.exp(c - cmax), axis=0, keepdims=True)

    n1 = jnp.zeros(s1.shape, F32)
    for j in range(PEER_TOPK):
        n1 = n1 + jnp.where(s1 + b[j:j + 1, :] >= thr, 1.0, 0.0)
    rank2_ref[...] = rank2.astype(BF16)
    p2_ref[...] = jnp.exp(s2 - b[0:1, :]).astype(BF16)
    n1_ref[...] = n1
    p1_ref[...] = jnp.exp(s1 - a[0:1, :]) * inv_z


def _route(h, wq, k1, k2, tm):
    t, d = h.shape
    hk = (PEER_HEADS, PEER_NKEYS, t)
    blk = lambda i, j: (j, 0, i)
    return pl.pallas_call(
        _route_kernel,
        grid=(t // tm, PEER_HEADS),
        in_specs=[
            pl.BlockSpec((tm, d), lambda i, j: (i, 0)),
            pl.BlockSpec((d, PEER_DQ), lambda i, j: (0, j)),
            pl.BlockSpec((PEER_NKEYS, PEER_DQ_HALF), lambda i, j: (0, 0)),
            pl.BlockSpec((PEER_NKEYS, PEER_DQ_HALF), lambda i, j: (0, 0)),
        ],
        out_specs=[
            pl.BlockSpec((d, tm), lambda i, j: (0, i)),
            pl.BlockSpec((None, PEER_NKEYS, tm), blk),
            pl.BlockSpec((None, PEER_NKEYS, tm), blk),
            pl.BlockSpec((None, PEER_NKEYS, tm), blk),
            pl.BlockSpec((None, PEER_NKEYS, tm), blk),
        ],
        out_shape=[
            jax.ShapeDtypeStruct((d, t), BF16),
            jax.ShapeDtypeStruct(hk, BF16),
            jax.ShapeDtypeStruct(hk, BF16),
            jax.ShapeDtypeStruct(hk, F32),
            jax.ShapeDtypeStruct(hk, F32),
        ],
        scratch_shapes=[pltpu.VMEM((PEER_TOPK, tm), F32),
                        pltpu.VMEM((PEER_TOPK, tm), F32),
                        pltpu.VMEM((PEER_TOPK, tm), F32)],
        compiler_params=_params(("parallel", "arbitrary")),
        name="peer_route",
    )(h, wq, k1, k2)


def _peer_kernel(xt_ref, u_ref, vt_ref, rank2_ref, p2_ref, n1_ref, p1_ref, h_ref, g_ref, b_ref,
                 out_ref, acc_ref, y_ref, *, rows):
    e = pl.program_id(1)

    @pl.when(e == 0)
    def _():
        acc_ref[...] = jnp.zeros_like(acc_ref)

    xt = xt_ref[...]
    for l in range(rows):
        es = slice(l * PEER_NKEYS, (l + 1) * PEER_NKEYS)
        act = _gelu(_dot(u_ref[es, :], xt)).astype(BF16)
        w = None
        for hd in range(PEER_HEADS):
            n1 = n1_ref[hd, l:l + 1, :].astype(BF16)
            p1 = p1_ref[hd, l:l + 1, :].astype(BF16)
            term = jnp.where(rank2_ref[hd] < n1, p2_ref[hd], jnp.zeros((), BF16)) * p1
            w = term if w is None else w + term
        y_ref[es, :] = act * w
    acc_ref[...] += _dot(vt_ref[...], y_ref[...])

    @pl.when(e == pl.num_programs(1) - 1)
    def _():
        ffn = jnp.transpose(acc_ref[...])
        out_ref[...] = _layer_norm(ALPHA * h_ref[...] + ffn, g_ref[...], b_ref[...])


def _peer(xt, u, vt, rank2, p2, n1, p1, h, g, b, tm, te):
    t, d = h.shape
    rows = te // PEER_NKEYS
    kern = functools.partial(_peer_kernel, rows=rows)
    route_blk = pl.BlockSpec((PEER_HEADS, PEER_NKEYS, tm), lambda i, e: (0, 0, i))
    row_blk = pl.BlockSpec((PEER_HEADS, rows, tm), lambda i, e: (0, e, i))
    return pl.pallas_call(
        kern,
        grid=(t // tm, PEER_EXPERTS // te),
        in_specs=[
            pl.BlockSpec((d, tm), lambda i, e: (0, i)),
            pl.BlockSpec((te, d), lambda i, e: (e, 0)),
            pl.BlockSpec((d, te), lambda i, e: (0, e)),
            route_blk, route_blk, row_blk, row_blk,
            pl.BlockSpec((tm, d), lambda i, e: (i, 0)),
            pl.BlockSpec((1, d), lambda i, e: (0, 0)),
            pl.BlockSpec((1, d), lambda i, e: (0, 0)),
        ],
        out_specs=pl.BlockSpec((tm, d), lambda i, e: (i, 0)),
        out_shape=jax.ShapeDtypeStruct((t, d), F32),
        scratch_shapes=[pltpu.VMEM((d, tm), F32), pltpu.VMEM((te, tm), BF16)],
        compiler_params=_params(("parallel", "arbitrary")),
        name="peer_dense",
    )(xt, u, vt, rank2, p2, n1, p1, h, g, b)


def _pack_w_in(w):
    q, k, v, r, a, su, sv = jnp.split(
        w, [256, 512, 1024, 1536, 1552, 2064], axis=-1)
    a = jnp.pad(a, ((0, 0), (0, A_PAD - GLA_GATE_RANK)))
    return jnp.concatenate([q, k, v, r, su, sv, a], axis=-1).astype(BF16)


def _forward(x, ln_in_g, ln_in_b, w_in, w_gate_up, b_gate, gla_norm_g, sgu_ln_g, sgu_ln_b,
             sgu_w, sgu_b, w_out, ln1_g, ln1_b, peer_wq, peer_k1, peer_k2, peer_u, peer_v,
             ln2_g, ln2_b, *, tm_proj, ts_gla, tm_mix, tm_route, tm_peer, te_peer):
    batch, seq, d = x.shape
    t = batch * seq
    row = lambda p: p.reshape(1, -1)
    h = _ln(x.reshape(t, d), ln_in_g, ln_in_b, tm_proj)
    for l in range(DEPTH):
        p = _in_proj(h, _pack_w_in(w_in[l]), tm_proj)
        wg = jnp.pad(w_gate_up[l], ((0, A_PAD - GLA_GATE_RANK), (0, 0)))
        o = _gla(p.reshape(batch, seq, P_COLS), wg, row(b_gate[l]), row(gla_norm_g[l]),
                 batch, seq, ts_gla)
        bs_full = jnp.broadcast_to(sgu_b[l][:, :, None],
                                   (SGU_GROUPS, SGU_WINDOW, SGU_GROUP_DIM))
        h = _mix_out(p, o.reshape(t, GLA_WIDTH), h, row(sgu_ln_g[l]), row(sgu_ln_b[l]),
                     sgu_w[l], bs_full, w_out[l].astype(BF16), row(ln1_g[l]), row(ln1_b[l]),
                     tm_mix)
        xt, rank2, p2, n1, p1 = _route(h, peer_wq[l].astype(BF16), peer_k1[l].astype(BF16),
                                       peer_k2[l].astype(BF16), tm_route)
        h = _peer(xt, peer_u[l].astype(BF16), jnp.transpose(peer_v[l]).astype(BF16),
                  rank2, p2, n1, p1, h, row(ln2_g[l]), row(ln2_b[l]), tm_peer, te_peer)
    return h.reshape(batch, seq, d)


def kernel(x, ln_in_g, ln_in_b, w_in, w_gate_up, b_gate, gla_norm_g, sgu_ln_g, sgu_ln_b, sgu_w,
           sgu_b, w_out, ln1_g, ln1_b, peer_wq, peer_k1, peer_k2, peer_u, peer_v, ln2_g, ln2_b):
    return _forward(x, ln_in_g, ln_in_b, w_in, w_gate_up, b_gate, gla_norm_g, sgu_ln_g,
                    sgu_ln_b, sgu_w, sgu_b, w_out, ln1_g, ln1_b, peer_wq, peer_k1, peer_k2,
                    peer_u, peer_v, ln2_g, ln2_b,
                    tm_proj=512, ts_gla=256, tm_mix=512, tm_route=256, tm_peer=512,
                    te_peer=1024)
```

```python
import functools
import math

import jax
import jax.numpy as jnp
from jax import lax
from jax.experimental import pallas as pl
from jax.experimental.pallas import tpu as pltpu

F32 = jnp.float32
BF16 = jnp.bfloat16

D_MODEL = 1024
DEPTH = 2
CHUNK = 64
GLA_HEADS = 4
GLA_DK = 64
GLA_DV = 128
GLA_WIDTH = GLA_HEADS * GLA_DV
GLA_QK = GLA_HEADS * GLA_DK
GLA_GATE_RANK = 16
GLA_GATE_NORMALIZER = 16.0
SGU_WIDTH = 512
SGU_GROUPS = 4
SGU_GROUP_DIM = 128
SGU_WINDOW = 128
PEER_HEADS = 8
PEER_NKEYS = 128
PEER_EXPERTS = PEER_NKEYS * PEER_NKEYS
PEER_TOPK = 16
PEER_DQ = 256
PEER_DQ_HALF = 128
LN_EPS = 1e-5
ALPHA = (2.0 * DEPTH) ** 0.25

LANES = 128
SUBLANES = 8

COL_Q = 0
COL_K = COL_Q + GLA_QK
COL_V = COL_K + GLA_QK
COL_R = COL_V + GLA_WIDTH
COL_SU = COL_R + GLA_WIDTH
COL_SV = COL_SU + SGU_WIDTH
COL_A = COL_SV + SGU_WIDTH
A_PAD = LANES
P_COLS = COL_A + A_PAD

SUB = 16
EXP_CLAMP = 80.0
NEG_INF = float("-inf")

VMEM_LIMIT = 56 * 1024 * 1024


def _params(sem):
    return pltpu.CompilerParams(dimension_semantics=sem, vmem_limit_bytes=VMEM_LIMIT)


def _layer_norm(x, g, b):
    mu = jnp.mean(x, axis=-1, keepdims=True)
    xc = x - mu
    var = jnp.mean(xc * xc, axis=-1, keepdims=True)
    return xc * lax.rsqrt(var + LN_EPS) * g + b


def _gelu(x):
    return 0.5 * x * (1.0 + lax.erf(x * (1.0 / math.sqrt(2.0))))


def _dot(a, b):
    return jnp.dot(a, b, preferred_element_type=F32)


def _dot_nt(a, b):
    return lax.dot_general(a, b, (((1,), (1,)), ((), ())), preferred_element_type=F32)


def _dot_tn(a, b):
    return lax.dot_general(a, b, (((0,), (0,)), ((), ())), preferred_element_type=F32)


def _ln_kernel(x_ref, g_ref, b_ref, o_ref):
    o_ref[...] = _layer_norm(x_ref[...], g_ref[...], b_ref[...])


def _ln(x, g, b, tm):
    t, d = x.shape
    return pl.pallas_call(
        _ln_kernel,
        grid=(t // tm,),
        in_specs=[pl.BlockSpec((tm, d), lambda i: (i, 0)),
                  pl.BlockSpec((1, d), lambda i: (0, 0)),
                  pl.BlockSpec((1, d), lambda i: (0, 0))],
        out_specs=pl.BlockSpec((tm, d), lambda i: (i, 0)),
        out_shape=jax.ShapeDtypeStruct((t, d), F32),
        compiler_params=_params(("parallel",)),
        name="ln_in",
    )(x, g.reshape(1, d), b.reshape(1, d))


def _in_proj_kernel(h_ref, w_ref, o_ref):
    o_ref[...] = _dot(h_ref[...].astype(BF16), w_ref[...])


def _in_proj(h, w, tm):
    t, d = h.shape
    n = w.shape[1]
    return pl.pallas_call(
        _in_proj_kernel,
        grid=(t // tm,),
        in_specs=[pl.BlockSpec((tm, d), lambda i: (i, 0)),
                  pl.BlockSpec((d, n), lambda i: (0, 0))],
        out_specs=pl.BlockSpec((tm, n), lambda i: (i, 0)),
        out_shape=jax.ShapeDtypeStruct((t, n), F32),
        compiler_params=_params(("parallel",)),
        name="in_proj",
    )(h, w)


def _gla_kernel(q_ref, k_ref, v_ref, r_ref, a_ref, wg_ref, bg_ref, ng_ref, o_ref, st_ref,
                *, chunks):
    @pl.when(pl.program_id(1) == 0)
    def _():
        st_ref[...] = jnp.zeros_like(st_ref)

    z = jnp.dot(a_ref[...], wg_ref[...], preferred_element_type=F32,
                precision=lax.Precision.HIGHEST) + bg_ref[...]
    logg = (jnp.minimum(z, 0.0) - jnp.log1p(jnp.exp(-jnp.abs(z)))) * (1.0 / GLA_GATE_NORMALIZER)

    row = lax.broadcasted_iota(jnp.int32, (CHUNK, CHUNK), 0)
    col = lax.broadcasted_iota(jnp.int32, (CHUNK, CHUNK), 1)
    causal = col <= row
    tril = causal.astype(F32)
    lane_head = lax.broadcasted_iota(jnp.int32, (1, LANES), 1) // GLA_DK
    ng = ng_ref[...]

    for c in range(chunks):
        rows = slice(c * CHUNK, (c + 1) * CHUNK)
        bc_all = jnp.dot(tril, logg[rows, :], preferred_element_type=F32,
                         precision=lax.Precision.HIGHEST)
        for grp in range(GLA_QK // LANES):
            ls = slice(grp * LANES, (grp + 1) * LANES)
            qg = q_ref[rows, ls] * (GLA_DK ** -0.5)
            kg = k_ref[rows, ls]
            bg = bc_all[:, ls]
            b_last = bg[CHUNK - 1:CHUNK, :]
            q_inter = (qg * jnp.exp(bg)).astype(BF16)
            kd = kg * jnp.exp(b_last - bg)
            decay = jnp.exp(b_last)
            att = [[], []]
            for i in range(CHUNK // SUB):
                lo, hi = i * SUB, (i + 1) * SUB
                ref = bg[lo:lo + 1, :]
                qi = qg[lo:hi, :] * jnp.exp(bg[lo:hi, :] - ref)
                ki = (kg * jnp.exp(jnp.minimum(ref - bg, EXP_CLAMP))).astype(BF16)
                q2 = jnp.concatenate([jnp.where(lane_head == 0, qi, 0.0),
                                      jnp.where(lane_head == 1, qi, 0.0)], axis=0).astype(BF16)
                s = _dot_nt(q2, ki)
                att[0].append(s[:SUB, :])
                att[1].append(s[SUB:, :])
            for sub in range(LANES // GLA_DK):
                h = grp * (LANES // GLA_DK) + sub
                vs = slice(h * GLA_DV, (h + 1) * GLA_DV)
                vh_b = v_ref[rows, vs].astype(BF16)
                a_h = jnp.where(causal, jnp.concatenate(att[sub], axis=0), 0.0).astype(BF16)
                state_t = st_ref[h]
                o = _dot_nt(q_inter, state_t.astype(BF16)) + _dot(a_h, vh_b)
                kd_h = jnp.where(lane_head == sub, kd, 0.0).astype(BF16)
                st_ref[h] = state_t * decay + _dot_tn(vh_b, kd_h)
                ms = jnp.mean(o * o, axis=-1, keepdims=True)
                on = o * lax.rsqrt(ms + LN_EPS) * ng
                rh = r_ref[rows, vs]
                o_ref[rows, vs] = on * (rh * (1.0 / (1.0 + jnp.exp(-rh))))


def _gla(p, wg, bg, ng, batch, seq, ts):
    kern = functools.partial(_gla_kernel, chunks=ts // CHUNK)
    return pl.pallas_call(
        kern,
        grid=(batch, seq // ts),
        in_specs=[
            pl.BlockSpec((None, ts, GLA_QK), lambda b, s: (b, s, COL_Q // GLA_QK)),
            pl.BlockSpec((None, ts, GLA_QK), lambda b, s: (b, s, COL_K // GLA_QK)),
            pl.BlockSpec((None, ts, GLA_WIDTH), lambda b, s: (b, s, COL_V // GLA_WIDTH)),
            pl.BlockSpec((None, ts, GLA_WIDTH), lambda b, s: (b, s, COL_R // GLA_WIDTH)),
            pl.BlockSpec((None, ts, A_PAD), lambda b, s: (b, s, COL_A // A_PAD)),
            pl.BlockSpec((A_PAD, GLA_QK), lambda b, s: (0, 0)),
            pl.BlockSpec((1, GLA_QK), lambda b, s: (0, 0)),
            pl.BlockSpec((1, GLA_DV), lambda b, s: (0, 0)),
        ],
        out_specs=pl.BlockSpec((None, ts, GLA_WIDTH), lambda b, s: (b, s, 0)),
        out_shape=jax.ShapeDtypeStruct((batch, seq, GLA_WIDTH), F32),
        scratch_shapes=[pltpu.VMEM((GLA_HEADS, GLA_DV, LANES), F32)],
        compiler_params=_params(("parallel", "arbitrary")),
        name="gla",
    )(p, p, p, p, p, wg, bg, ng)


def _mix_out_kernel(su_ref, sv_ref, o_ref, h_ref, lg_ref, lb_ref, ws_ref, bs_ref, wo_ref,
                    g1_ref, b1_ref, out_ref, gate_ref, *, windows):
    u = _gelu(su_ref[...])
    v = _layer_norm(_gelu(sv_ref[...]), lg_ref[...], lb_ref[...])
    blk_r = lax.broadcasted_iota(jnp.int32, (SGU_WINDOW, SGU_WINDOW), 0) // CHUNK
    blk_c = lax.broadcasted_iota(jnp.int32, (SGU_WINDOW, SGU_WINDOW), 1) // CHUNK
    keep = blk_c <= blk_r
    for g in range(SGU_GROUPS):
        cs = slice(g * SGU_GROUP_DIM, (g + 1) * SGU_GROUP_DIM)
        w = jnp.where(keep, ws_ref[g], 0.0).astype(BF16)
        bias = bs_ref[g]
        for n in range(windows):
            rs = slice(n * SGU_WINDOW, (n + 1) * SGU_WINDOW)
            sv = _dot(w, v[rs, cs].astype(BF16)) + bias
            gate_ref[rs, cs] = u[rs, cs] * sv
    mix = _dot(o_ref[...].astype(BF16), wo_ref[0:GLA_WIDTH, :])
    mix = mix + _dot(gate_ref[...].astype(BF16), wo_ref[GLA_WIDTH:, :])
    out_ref[...] = _layer_norm(ALPHA * h_ref[...] + mix, g1_ref[...], b1_ref[...])


def _mix_out(p, o, h, lg, lb, ws, bs_full, wo, g1, b1, tm):
    t, d = h.shape
    kern = functools.partial(_mix_out_kernel, windows=tm // SGU_WINDOW)
    const2 = lambda i: (0, 0)
    const3 = lambda i: (0, 0, 0)
    return pl.pallas_call(
        kern,
        grid=(t // tm,),
        in_specs=[
            pl.BlockSpec((tm, SGU_WIDTH), lambda i: (i, COL_SU // SGU_WIDTH)),
            pl.BlockSpec((tm, SGU_WIDTH), lambda i: (i, COL_SV // SGU_WIDTH)),
            pl.BlockSpec((tm, GLA_WIDTH), lambda i: (i, 0)),
            pl.BlockSpec((tm, d), lambda i: (i, 0)),
            pl.BlockSpec((1, SGU_WIDTH), const2),
            pl.BlockSpec((1, SGU_WIDTH), const2),
            pl.BlockSpec((SGU_GROUPS, SGU_WINDOW, SGU_WINDOW), const3),
            pl.BlockSpec((SGU_GROUPS, SGU_WINDOW, SGU_GROUP_DIM), const3),
            pl.BlockSpec((d, d), const2),
            pl.BlockSpec((1, d), const2),
            pl.BlockSpec((1, d), const2),
        ],
        out_specs=pl.BlockSpec((tm, d), lambda i: (i, 0)),
        out_shape=jax.ShapeDtypeStruct((t, d), F32),
        scratch_shapes=[pltpu.VMEM((tm, SGU_WIDTH), F32)],
        compiler_params=_params(("parallel",)),
        name="mix_out",
    )(p, p, o, h, lg, lb, ws, bs_full, wo, g1, b1)


def _sort_network(n):
    pairs = []

    def merge(lo, hi, r):
        step = r * 2
        if step < hi - lo:
            merge(lo, hi, step)
            merge(lo + r, hi, step)
            pairs.extend((i, i + r) for i in range(lo + r, hi - r, step))
        else:
            pairs.append((lo, lo + r))

    def sort(lo, hi):
        if hi - lo >= 1:
            mid = lo + (hi - lo) // 2
            sort(lo, mid)
            sort(mid + 1, hi)
            merge(lo, hi, 1)

    sort(0, n - 1)
    return pairs


def _ce(x, y):
    if y is None:
        return x, None
    if x is None:
        return y, None
    return jnp.maximum(x, y), jnp.minimum(x, y)


def _merge_top(a, b):
    n = len(a)
    v = [_ce(a[k], b[n - 1 - k])[0] for k in range(n)]
    d = n // 2
    while d >= 1:
        for i in range(n):
            if i & d == 0:
                v[i], v[i + d] = _ce(v[i], v[i + d])
        d //= 2
    return v


def _top_sorted(tiles):
    v = list(tiles)
    for i, j in _sort_network(len(v)):
        v[i], v[j] = _ce(v[i], v[j])
    shift = SUBLANES // 2
    while shift >= 1:
        v = _merge_top(v, [pltpu.roll(x, shift, axis=0) for x in v])
        shift //= 2
    return v


def _pad(vals):
    return list(vals) + [None] * (PEER_TOPK - len(vals))


def _route_kernel(h_ref, wq_ref, k1_ref, k2_ref, xt_ref, rank2_ref, p2_ref, n1_ref, p1_ref,
                  xb_ref, s1_ref, s2_ref):
    tm = h_ref.shape[0]
    x = h_ref[...]
    xt_ref[...] = jnp.transpose(x).astype(BF16)
    xb_ref[...] = x.astype(BF16)
    ntile = PEER_NKEYS // SUBLANES

    def head_body(hd, carry):
        q = _dot(xb_ref[...], wq_ref[hd])
        s1_ref[...] = _dot_nt(k1_ref[...], q[:, :PEER_DQ_HALF].astype(BF16))
        s2_ref[...] = _dot_nt(k2_ref[...], q[:, PEER_DQ_HALF:].astype(BF16))
        for c in range(tm // LANES):
            ls = slice(c * LANES, (c + 1) * LANES)
            tile = lambda ref, k: ref[k * SUBLANES:(k + 1) * SUBLANES, ls]
            a = _top_sorted([tile(s1_ref, k) for k in range(ntile)])
            b = _top_sorted([tile(s2_ref, k) for k in range(ntile)])
            row = [_pad([a[i] + b[j] for j in range(PEER_TOPK // (i + 1))])
                   for i in range(SUBLANES)]
            tail = _pad([a[i] + b[0] for i in range(SUBLANES, PEER_TOPK)])
            small = _merge_top(_merge_top(row[4], row[5]), _merge_top(row[6], row[7]))
            mid = _merge_top(_merge_top(row[2], row[3]), small)
            top = _merge_top(row[0], _merge_top(_merge_top(row[1], tail), mid))
            thr = top[PEER_TOPK - 1]
            z = None
            for r in range(PEER_TOPK):
                e = jnp.exp(top[r] - top[0])
                z = e if z is None else z + e
            inv_z = 1.0 / z
            cut = []
            for j in range(PEER_TOPK):
                cj = jnp.full_like(thr, float("inf"))
                for i in range(min(PEER_TOPK // (j + 1), SUBLANES)):
                    cj = jnp.where(row[i][j] >= thr, a[i], cj)
                if j == 0:
                    for i in range(SUBLANES, PEER_TOPK):
                        cj = jnp.where(tail[i - SUBLANES] >= thr, a[i], cj)
                cut.append(cj)
            for kp in range(ntile // 2):
                r2, pp2 = [], []
                for k in (2 * kp, 2 * kp + 1):
                    rows = slice(k * SUBLANES, (k + 1) * SUBLANES)
                    s1k = tile(s1_ref, k)
                    n1 = jnp.zeros_like(s1k)
                    for j in range(PEER_TOPK):
                        n1 = jnp.where(s1k >= cut[j], float(j + 1), n1)
                    n1_ref[hd, rows, ls] = n1
                    p1_ref[hd, rows, ls] = jnp.exp(s1k - a[0]) * inv_z
                    s2k = tile(s2_ref, k)
                    rk = jnp.full_like(s2k, float(PEER_TOPK))
                    for r in range(PEER_TOPK - 1, -1, -1):
                        rk = jnp.where(s2k >= b[r], float(r), rk)
                    r2.append(rk)
                    pp2.append(jnp.exp(s2k - b[0]))
                rows2 = slice(kp * 2 * SUBLANES, (kp + 1) * 2 * SUBLANES)
                rank2_ref[hd, rows2, ls] = jnp.concatenate(r2, axis=0).astype(BF16)
                p2_ref[hd, rows2, ls] = jnp.concatenate(pp2, axis=0).astype(BF16)
        return carry

    lax.fori_loop(0, PEER_HEADS, head_body, 0)


def _route(h, wq, k1, k2, tm):
    t, d = h.shape
    hk = (PEER_HEADS, PEER_NKEYS, t)
    blk = pl.BlockSpec((PEER_HEADS, PEER_NKEYS, tm), lambda i: (0, 0, i))
    return pl.pallas_call(
        _route_kernel,
        grid=(t // tm,),
        in_specs=[
            pl.BlockSpec((tm, d), lambda i: (i, 0)),
            pl.BlockSpec((PEER_HEADS, d, PEER_DQ), lambda i: (0, 0, 0)),
            pl.BlockSpec((PEER_NKEYS, PEER_DQ_HALF), lambda i: (0, 0)),
            pl.BlockSpec((PEER_NKEYS, PEER_DQ_HALF), lambda i: (0, 0)),
        ],
        out_specs=[pl.BlockSpec((d, tm), lambda i: (0, i)), blk, blk, blk, blk],
        out_shape=[
            jax.ShapeDtypeStruct((d, t), BF16),
            jax.ShapeDtypeStruct(hk, BF16),
            jax.ShapeDtypeStruct(hk, BF16),
            jax.ShapeDtypeStruct(hk, F32),
            jax.ShapeDtypeStruct(hk, F32),
        ],
        scratch_shapes=[pltpu.VMEM((tm, d), BF16),
                        pltpu.VMEM((PEER_NKEYS, tm), F32),
                        pltpu.VMEM((PEER_NKEYS, tm), F32)],
        compiler_params=_params(("parallel",)),
        name="peer_route",
    )(h, wq, k1, k2)


def _peer_kernel(xt_ref, u_ref, vt_ref, rank2_ref, p2_ref, n1_ref, p1_ref, h_ref, g_ref, b_ref,
                 out_ref, acc_ref, y_ref, *, rows):
    e = pl.program_id(1)

    @pl.when(e == 0)
    def _():
        acc_ref[...] = jnp.zeros_like(acc_ref)

    xt = xt_ref[...]
    for l in range(rows):
        es = slice(l * PEER_NKEYS, (l + 1) * PEER_NKEYS)
        act = _gelu(_dot(u_ref[es, :], xt)).astype(BF16)
        w = None
        for hd in range(PEER_HEADS):
            n1 = n1_ref[hd, l:l + 1, :].astype(BF16)
            p1 = p1_ref[hd, l:l + 1, :].astype(BF16)
            term = jnp.where(rank2_ref[hd] < n1, p2_ref[hd], jnp.zeros((), BF16)) * p1
            w = term if w is None else w + term
        y_ref[es, :] = act * w
    acc_ref[...] += _dot(vt_ref[...], y_ref[...])

    @pl.when(e == pl.num_programs(1) - 1)
    def _():
        ffn = jnp.transpose(acc_ref[...])
        out_ref[...] = _layer_norm(ALPHA * h_ref[...] + ffn, g_ref[...], b_ref[...])


def _peer(xt, u, vt, rank2, p2, n1, p1, h, g, b, tm, te):
    t, d = h.shape
    rows = te // PEER_NKEYS
    kern = functools.partial(_peer_kernel, rows=rows)
    route_blk = pl.BlockSpec((PEER_HEADS, PEER_NKEYS, tm), lambda i, e: (0, 0, i))
    row_blk = pl.BlockSpec((PEER_HEADS, rows, tm), lambda i, e: (0, e, i))
    return pl.pallas_call(
        kern,
        grid=(t // tm, PEER_EXPERTS // te),
        in_specs=[
            pl.BlockSpec((d, tm), lambda i, e: (0, i)),
            pl.BlockSpec((te, d), lambda i, e: (e, 0)),
            pl.BlockSpec((d, te), lambda i, e: (0, e)),
            route_blk, route_blk, row_blk, row_blk,
            pl.BlockSpec((tm, d), lambda i, e: (i, 0)),
            pl.BlockSpec((1, d), lambda i, e: (0, 0)),
            pl.BlockSpec((1, d), lambda i, e: (0, 0)),
        ],
        out_specs=pl.BlockSpec((tm, d), lambda i, e: (i, 0)),
        out_shape=jax.ShapeDtypeStruct((t, d), F32),
        scratch_shapes=[pltpu.VMEM((d, tm), F32), pltpu.VMEM((te, tm), BF16)],
        compiler_params=_params(("parallel", "arbitrary")),
        name="peer_dense",
    )(xt, u, vt, rank2, p2, n1, p1, h, g, b)


def _pack_w_in(w):
    q, k, v, r, a, su, sv = jnp.split(
        w, [256, 512, 1024, 1536, 1552, 2064], axis=-1)
    a = jnp.pad(a, ((0, 0), (0, A_PAD - GLA_GATE_RANK)))
    return jnp.concatenate([q, k, v, r, su, sv, a], axis=-1).astype(BF16)


def _forward(x, ln_in_g, ln_in_b, w_in, w_gate_up, b_gate, gla_norm_g, sgu_ln_g, sgu_ln_b,
             sgu_w, sgu_b, w_out, ln1_g, ln1_b, peer_wq, peer_k1, peer_k2, peer_u, peer_v,
             ln2_g, ln2_b, *, tm_proj, ts_gla, tm_mix, tm_route, tm_peer, te_peer):
    batch, seq, d = x.shape
    t = batch * seq
    row = lambda p: p.reshape(1, -1)
    h = _ln(x.reshape(t, d), ln_in_g, ln_in_b, tm_proj)
    for l in range(DEPTH):
        p = _in_proj(h, _pack_w_in(w_in[l]), tm_proj)
        wg = jnp.pad(w_gate_up[l], ((0, A_PAD - GLA_GATE_RANK), (0, 0)))
        o = _gla(p.reshape(batch, seq, P_COLS), wg, row(b_gate[l]), row(gla_norm_g[l]),
                 batch, seq, ts_gla)
        bs_full = jnp.broadcast_to(sgu_b[l][:, :, None],
                                   (SGU_GROUPS, SGU_WINDOW, SGU_GROUP_DIM))
        h = _mix_out(p, o.reshape(t, GLA_WIDTH), h, row(sgu_ln_g[l]), row(sgu_ln_b[l]),
                     sgu_w[l], bs_full, w_out[l].astype(BF16), row(ln1_g[l]), row(ln1_b[l]),
                     tm_mix)
        wq = peer_wq[l].reshape(d, PEER_HEADS, PEER_DQ).transpose(1, 0, 2).astype(BF16)
        xt, rank2, p2, n1, p1 = _route(h, wq, peer_k1[l].astype(BF16),
                                       peer_k2[l].astype(BF16), tm_route)
        h = _peer(xt, peer_u[l].astype(BF16), jnp.transpose(peer_v[l]).astype(BF16),
                  rank2, p2, n1, p1, h, row(ln2_g[l]), row(ln2_b[l]), tm_peer, te_peer)
    return h.reshape(batch, seq, d)


def kernel(x, ln_in_g, ln_in_b, w_in, w_gate_up, b_gate, gla_norm_g, sgu_ln_g, sgu_ln_b, sgu_w,
           sgu_b, w_out, ln1_g, ln1_b, peer_wq, peer_k1, peer_k2, peer_u, peer_v, ln2_g, ln2_b):
    return _forward(x, ln_in_g, ln_in_b, w_in, w_gate_up, b_gate, gla_norm_g, sgu_ln_g,
                    sgu_ln_b, sgu_w, sgu_b, w_out, ln1_g, ln1_b, peer_wq, peer_k1, peer_k2,
                    peer_u, peer_v, ln2_g, ln2_b,
                    tm_proj=512, ts_gla=256, tm_mix=512, tm_route=512, tm_peer=512,
                    te_peer=1024)
```

```python
import functools
import math

import jax
import jax.numpy as jnp
from jax import lax
from jax.experimental import pallas as pl
from jax.experimental.pallas import tpu as pltpu

F32 = jnp.float32
BF16 = jnp.bfloat16

D_MODEL = 1024
DEPTH = 2
CHUNK = 64
GLA_HEADS = 4
GLA_DK = 64
GLA_DV = 128
GLA_WIDTH = GLA_HEADS * GLA_DV
GLA_QK = GLA_HEADS * GLA_DK
GLA_GATE_RANK = 16
GLA_GATE_NORMALIZER = 16.0
SGU_WIDTH = 512
SGU_GROUPS = 4
SGU_GROUP_DIM = 128
SGU_WINDOW = 128
PEER_HEADS = 8
PEER_NKEYS = 128
PEER_EXPERTS = PEER_NKEYS * PEER_NKEYS
PEER_TOPK = 16
PEER_DQ = 256
PEER_DQ_HALF = 128
LN_EPS = 1e-5
ALPHA = (2.0 * DEPTH) ** 0.25

LANES = 128
SUBLANES = 8

COL_Q = 0
COL_K = COL_Q + GLA_QK
COL_V = COL_K + GLA_QK
COL_R = COL_V + GLA_WIDTH
COL_SU = COL_R + GLA_WIDTH
COL_SV = COL_SU + SGU_WIDTH
COL_A = COL_SV + SGU_WIDTH
A_PAD = LANES
P_COLS = COL_A + A_PAD

SUB = 16
EXP_CLAMP = 80.0
NEG_INF = float("-inf")

VMEM_LIMIT = 56 * 1024 * 1024


def _params(sem, flags=None):
    return pltpu.CompilerParams(dimension_semantics=sem, vmem_limit_bytes=VMEM_LIMIT, flags=flags)


def _layer_norm(x, g, b):
    mu = jnp.mean(x, axis=-1, keepdims=True)
    xc = x - mu
    var = jnp.mean(xc * xc, axis=-1, keepdims=True)
    return xc * lax.rsqrt(var + LN_EPS) * g + b


def _gelu(x):
    return 0.5 * x * (1.0 + lax.erf(x * (1.0 / math.sqrt(2.0))))


def _dot(a, b):
    return jnp.dot(a, b, preferred_element_type=F32)


def _dot_nt(a, b):
    return lax.dot_general(a, b, (((1,), (1,)), ((), ())), preferred_element_type=F32)


def _dot_tn(a, b):
    return lax.dot_general(a, b, (((0,), (0,)), ((), ())), preferred_element_type=F32)


def _ln_kernel(x_ref, g_ref, b_ref, o_ref):
    o_ref[...] = _layer_norm(x_ref[...], g_ref[...], b_ref[...])


def _ln(x, g, b, tm):
    t, d = x.shape
    return pl.pallas_call(
        _ln_kernel,
        grid=(t // tm,),
        in_specs=[pl.BlockSpec((tm, d), lambda i: (i, 0)),
                  pl.BlockSpec((1, d), lambda i: (0, 0)),
                  pl.BlockSpec((1, d), lambda i: (0, 0))],
        out_specs=pl.BlockSpec((tm, d), lambda i: (i, 0)),
        out_shape=jax.ShapeDtypeStruct((t, d), F32),
        compiler_params=_params(("parallel",)),
        name="ln_in",
    )(x, g.reshape(1, d), b.reshape(1, d))


def _in_proj_kernel(h_ref, w_ref, o_ref):
    o_ref[...] = _dot(h_ref[...].astype(BF16), w_ref[...])


def _in_proj(h, w, tm):
    t, d = h.shape
    n = w.shape[1]
    return pl.pallas_call(
        _in_proj_kernel,
        grid=(t // tm,),
        in_specs=[pl.BlockSpec((tm, d), lambda i: (i, 0)),
                  pl.BlockSpec((d, n), lambda i: (0, 0))],
        out_specs=pl.BlockSpec((tm, n), lambda i: (i, 0)),
        out_shape=jax.ShapeDtypeStruct((t, n), F32),
        compiler_params=_params(("parallel",)),
        name="in_proj",
    )(h, w)


def _gla_kernel(q_ref, k_ref, v_ref, r_ref, a_ref, wg_ref, bg_ref, ng_ref, o_ref, st_ref,
                *, chunks):
    @pl.when(pl.program_id(1) == 0)
    def _():
        st_ref[...] = jnp.zeros_like(st_ref)

    z = jnp.dot(a_ref[...], wg_ref[...], preferred_element_type=F32,
                precision=lax.Precision.HIGHEST) + bg_ref[...]
    logg = (jnp.minimum(z, 0.0) - jnp.log1p(jnp.exp(-jnp.abs(z)))) * (1.0 / GLA_GATE_NORMALIZER)

    row = lax.broadcasted_iota(jnp.int32, (CHUNK, CHUNK), 0)
    col = lax.broadcasted_iota(jnp.int32, (CHUNK, CHUNK), 1)
    causal = col <= row
    tril = causal.astype(F32)
    lane_head = lax.broadcasted_iota(jnp.int32, (1, LANES), 1) // GLA_DK
    ng = ng_ref[...]

    for c in range(chunks):
        rows = slice(c * CHUNK, (c + 1) * CHUNK)
        bc_all = jnp.dot(tril, logg[rows, :], preferred_element_type=F32,
                         precision=lax.Precision.HIGHEST)
        for grp in range(GLA_QK // LANES):
            ls = slice(grp * LANES, (grp + 1) * LANES)
            qg = q_ref[rows, ls] * (GLA_DK ** -0.5)
            kg = k_ref[rows, ls]
            bg = bc_all[:, ls]
            b_last = bg[CHUNK - 1:CHUNK, :]
            q_inter = (qg * jnp.exp(bg)).astype(BF16)
            kd = kg * jnp.exp(b_last - bg)
            decay = jnp.exp(b_last)
            att = [[], []]
            for i in range(CHUNK // SUB):
                lo, hi = i * SUB, (i + 1) * SUB
                ref = bg[lo:lo + 1, :]
                qi = qg[lo:hi, :] * jnp.exp(bg[lo:hi, :] - ref)
                ki = (kg * jnp.exp(jnp.minimum(ref - bg, EXP_CLAMP))).astype(BF16)
                q2 = jnp.concatenate([jnp.where(lane_head == 0, qi, 0.0),
                                      jnp.where(lane_head == 1, qi, 0.0)], axis=0).astype(BF16)
                s = _dot_nt(q2, ki)
                att[0].append(s[:SUB, :])
                att[1].append(s[SUB:, :])
            for sub in range(LANES // GLA_DK):
                h = grp * (LANES // GLA_DK) + sub
                vs = slice(h * GLA_DV, (h + 1) * GLA_DV)
                vh_b = v_ref[rows, vs].astype(BF16)
                a_h = jnp.where(causal, jnp.concatenate(att[sub], axis=0), 0.0).astype(BF16)
                state_t = st_ref[h]
                o = _dot_nt(q_inter, state_t.astype(BF16)) + _dot(a_h, vh_b)
                kd_h = jnp.where(lane_head == sub, kd, 0.0).astype(BF16)
                st_ref[h] = state_t * decay + _dot_tn(vh_b, kd_h)
                ms = jnp.mean(o * o, axis=-1, keepdims=True)
                on = o * lax.rsqrt(ms + LN_EPS) * ng
                rh = r_ref[rows, vs]
                o_ref[rows, vs] = on * (rh * (1.0 / (1.0 + jnp.exp(-rh))))


def _gla(p, wg, bg, ng, batch, seq, ts):
    kern = functools.partial(_gla_kernel, chunks=ts // CHUNK)
    return pl.pallas_call(
        kern,
        grid=(batch, seq // ts),
        in_specs=[
            pl.BlockSpec((None, ts, GLA_QK), lambda b, s: (b, s, COL_Q // GLA_QK)),
            pl.BlockSpec((None, ts, GLA_QK), lambda b, s: (b, s, COL_K // GLA_QK)),
            pl.BlockSpec((None, ts, GLA_WIDTH), lambda b, s: (b, s, COL_V // GLA_WIDTH)),
            pl.BlockSpec((None, ts, GLA_WIDTH), lambda b, s: (b, s, COL_R // GLA_WIDTH)),
            pl.BlockSpec((None, ts, A_PAD), lambda b, s: (b, s, COL_A // A_PAD)),
            pl.BlockSpec((A_PAD, GLA_QK), lambda b, s: (0, 0)),
            pl.BlockSpec((1, GLA_QK), lambda b, s: (0, 0)),
            pl.BlockSpec((1, GLA_DV), lambda b, s: (0, 0)),
        ],
        out_specs=pl.BlockSpec((None, ts, GLA_WIDTH), lambda b, s: (b, s, 0)),
        out_shape=jax.ShapeDtypeStruct((batch, seq, GLA_WIDTH), F32),
        scratch_shapes=[pltpu.VMEM((GLA_HEADS, GLA_DV, LANES), F32)],
        compiler_params=_params(("parallel", "arbitrary")),
        name="gla",
    )(p, p, p, p, p, wg, bg, ng)


def _mix_out_kernel(su_ref, sv_ref, o_ref, h_ref, lg_ref, lb_ref, ws_ref, bs_ref, wo_ref,
                    g1_ref, b1_ref, out_ref, gate_ref, *, windows):
    u = _gelu(su_ref[...])
    v = _layer_norm(_gelu(sv_ref[...]), lg_ref[...], lb_ref[...])
    blk_r = lax.broadcasted_iota(jnp.int32, (SGU_WINDOW, SGU_WINDOW), 0) // CHUNK
    blk_c = lax.broadcasted_iota(jnp.int32, (SGU_WINDOW, SGU_WINDOW), 1) // CHUNK
    keep = blk_c <= blk_r
    for g in range(SGU_GROUPS):
        cs = slice(g * SGU_GROUP_DIM, (g + 1) * SGU_GROUP_DIM)
        w = jnp.where(keep, ws_ref[g], 0.0).astype(BF16)
        bias = bs_ref[g]
        for n in range(windows):
            rs = slice(n * SGU_WINDOW, (n + 1) * SGU_WINDOW)
            sv = _dot(w, v[rs, cs].astype(BF16)) + bias
            gate_ref[rs, cs] = u[rs, cs] * sv
    mix = _dot(o_ref[...].astype(BF16), wo_ref[0:GLA_WIDTH, :])
    mix = mix + _dot(gate_ref[...].astype(BF16), wo_ref[GLA_WIDTH:, :])
    out_ref[...] = _layer_norm(ALPHA * h_ref[...] + mix, g1_ref[...], b1_ref[...])


def _mix_out(p, o, h, lg, lb, ws, bs_full, wo, g1, b1, tm):
    t, d = h.shape
    kern = functools.partial(_mix_out_kernel, windows=tm // SGU_WINDOW)
    const2 = lambda i: (0, 0)
    const3 = lambda i: (0, 0, 0)
    return pl.pallas_call(
        kern,
        grid=(t // tm,),
        in_specs=[
            pl.BlockSpec((tm, SGU_WIDTH), lambda i: (i, COL_SU // SGU_WIDTH)),
            pl.BlockSpec((tm, SGU_WIDTH), lambda i: (i, COL_SV // SGU_WIDTH)),
            pl.BlockSpec((tm, GLA_WIDTH), lambda i: (i, 0)),
            pl.BlockSpec((tm, d), lambda i: (i, 0)),
            pl.BlockSpec((1, SGU_WIDTH), const2),
            pl.BlockSpec((1, SGU_WIDTH), const2),
            pl.BlockSpec((SGU_GROUPS, SGU_WINDOW, SGU_WINDOW), const3),
            pl.BlockSpec((SGU_GROUPS, SGU_WINDOW, SGU_GROUP_DIM), const3),
            pl.BlockSpec((d, d), const2),
            pl.BlockSpec((1, d), const2),
            pl.BlockSpec((1, d), const2),
        ],
        out_specs=pl.BlockSpec((tm, d), lambda i: (i, 0)),
        out_shape=jax.ShapeDtypeStruct((t, d), F32),
        scratch_shapes=[pltpu.VMEM((tm, SGU_WIDTH), F32)],
        compiler_params=_params(("parallel",)),
        name="mix_out",
    )(p, p, o, h, lg, lb, ws, bs_full, wo, g1, b1)


def _sort_network(n):
    pairs = []

    def merge(lo, hi, r):
        step = r * 2
        if step < hi - lo:
            merge(lo, hi, step)
            merge(lo + r, hi, step)
            pairs.extend((i, i + r) for i in range(lo + r, hi - r, step))
        else:
            pairs.append((lo, lo + r))

    def sort(lo, hi):
        if hi - lo >= 1:
            mid = lo + (hi - lo) // 2
            sort(lo, mid)
            sort(mid + 1, hi)
            merge(lo, hi, 1)

    sort(0, n - 1)
    return pairs


def _ce(x, y):
    if y is None:
        return x, None
    if x is None:
        return y, None
    return jnp.maximum(x, y), jnp.minimum(x, y)


def _merge_top(a, b):
    n = len(a)
    v = [_ce(a[k], b[n - 1 - k])[0] for k in range(n)]
    d = n // 2
    while d >= 1:
        for i in range(n):
            if i & d == 0:
                v[i], v[i + d] = _ce(v[i], v[i + d])
        d //= 2
    return v


def _top_sorted(tiles):
    v = list(tiles)
    for i, j in _sort_network(len(v)):
        v[i], v[j] = _ce(v[i], v[j])
    shift = SUBLANES // 2
    while shift >= 1:
        v = _merge_top(v, [pltpu.roll(x, shift, axis=0) for x in v])
        shift //= 2
    return v


def _pad(vals):
    return list(vals) + [None] * (PEER_TOPK - len(vals))


def _route_kernel(h_ref, wq_ref, k1_ref, k2_ref, xt_ref, rank2_ref, p2_ref, n1_ref, p1_ref,
                  xb_ref, s1_ref, s2_ref):
    tm = h_ref.shape[0]
    x = h_ref[...]
    xt_ref[...] = jnp.transpose(x).astype(BF16)
    xb_ref[...] = x.astype(BF16)
    ntile = PEER_NKEYS // SUBLANES

    def head_body(hd, carry):
        q = _dot(xb_ref[...], wq_ref[hd])
        s1_ref[...] = _dot_nt(k1_ref[...], q[:, :PEER_DQ_HALF].astype(BF16))
        s2_ref[...] = _dot_nt(k2_ref[...], q[:, PEER_DQ_HALF:].astype(BF16))
        for c in range(tm // LANES):
            ls = slice(c * LANES, (c + 1) * LANES)
            tile = lambda ref, k: ref[k * SUBLANES:(k + 1) * SUBLANES, ls]
            a = _top_sorted([tile(s1_ref, k) for k in range(ntile)])
            b = _top_sorted([tile(s2_ref, k) for k in range(ntile)])
            row = [_pad([a[i] + b[j] for j in range(PEER_TOPK // (i + 1))])
                   for i in range(SUBLANES)]
            tail = _pad([a[i] + b[0] for i in range(SUBLANES, PEER_TOPK)])
            small = _merge_top(_merge_top(row[4], row[5]), _merge_top(row[6], row[7]))
            mid = _merge_top(_merge_top(row[2], row[3]), small)
            top = _merge_top(row[0], _merge_top(_merge_top(row[1], tail), mid))
            thr = top[PEER_TOPK - 1]
            z = None
            for r in range(PEER_TOPK):
                e = jnp.exp(top[r] - top[0])
                z = e if z is None else z + e
            inv_z = 1.0 / z
            cut = []
            for j in range(PEER_TOPK):
                cj = jnp.full_like(thr, float("inf"))
                for i in range(min(PEER_TOPK // (j + 1), SUBLANES)):
                    cj = jnp.where(row[i][j] >= thr, a[i], cj)
                if j == 0:
                    for i in range(SUBLANES, PEER_TOPK):
                        cj = jnp.where(tail[i - SUBLANES] >= thr, a[i], cj)
                cut.append(cj)
            for kp in range(ntile // 2):
                r2, pp2 = [], []
                for k in (2 * kp, 2 * kp + 1):
                    rows = slice(k * SUBLANES, (k + 1) * SUBLANES)
                    s1k = tile(s1_ref, k)
                    n1 = jnp.zeros_like(s1k)
                    for j in range(PEER_TOPK):
                        n1 = jnp.where(s1k >= cut[j], float(j + 1), n1)
                    n1_ref[hd, rows, ls] = n1
                    p1_ref[hd, rows, ls] = jnp.exp(s1k - a[0]) * inv_z
                    s2k = tile(s2_ref, k)
                    rk = jnp.full_like(s2k, float(PEER_TOPK))
                    for r in range(PEER_TOPK - 1, -1, -1):
                        rk = jnp.where(s2k >= b[r], float(r), rk)
                    r2.append(rk)
                    pp2.append(jnp.exp(s2k - b[0]))
                rows2 = slice(kp * 2 * SUBLANES, (kp + 1) * 2 * SUBLANES)
                rank2_ref[hd, rows2, ls] = jnp.concatenate(r2, axis=0).astype(BF16)
                p2_ref[hd, rows2, ls] = jnp.concatenate(pp2, axis=0).astype(BF16)
        return carry

    lax.fori_loop(0, PEER_HEADS, head_body, 0)


def _route(h, wq, k1, k2, tm):
    t, d = h.shape
    hk = (PEER_HEADS, PEER_NKEYS, t)
    blk = pl.BlockSpec((PEER_HEADS, PEER_NKEYS, tm), lambda i: (0, 0, i))
    return pl.pallas_call(
        _route_kernel,
        grid=(t // tm,),
        in_specs=[
            pl.BlockSpec((tm, d), lambda i: (i, 0)),
            pl.BlockSpec((PEER_HEADS, d, PEER_DQ), lambda i: (0, 0, 0)),
            pl.BlockSpec((PEER_NKEYS, PEER_DQ_HALF), lambda i: (0, 0)),
            pl.BlockSpec((PEER_NKEYS, PEER_DQ_HALF), lambda i: (0, 0)),
        ],
        out_specs=[pl.BlockSpec((d, tm), lambda i: (0, i)), blk, blk, blk, blk],
        out_shape=[
            jax.ShapeDtypeStruct((d, t), BF16),
            jax.ShapeDtypeStruct(hk, BF16),
            jax.ShapeDtypeStruct(hk, BF16),
            jax.ShapeDtypeStruct(hk, F32),
            jax.ShapeDtypeStruct(hk, F32),
        ],
        scratch_shapes=[pltpu.VMEM((tm, d), BF16),
                        pltpu.VMEM((PEER_NKEYS, tm), F32),
                        pltpu.VMEM((PEER_NKEYS, tm), F32)],
        compiler_params=_params(("parallel",)),
        name="peer_route",
    )(h, wq, k1, k2)


def _peer_kernel(xt_ref, u_ref, vt_ref, rank2_ref, p2_ref, n1_ref, p1_ref, h_ref, g_ref, b_ref,
                 out_ref, acc_ref, y_ref, *, rows):
    e = pl.program_id(1)

    @pl.when(e == 0)
    def _():
        acc_ref[...] = jnp.zeros_like(acc_ref)

    xt = xt_ref[...]
    tm = xt.shape[1]
    pack = 2 * SUBLANES
    slabs = PEER_NKEYS // pack
    zero = jnp.zeros((), BF16)

    def row_tile(ref, hd, l):
        r8 = jnp.broadcast_to(ref[hd, l:l + 1, :], (SUBLANES, tm))
        return jnp.concatenate([r8, r8], axis=0).astype(BF16)[None]

    for l in range(rows):
        es = slice(l * PEER_NKEYS, (l + 1) * PEER_NKEYS)
        act = _gelu(_dot(u_ref[es, :], xt).astype(BF16))
        w = None
        for hd in range(PEER_HEADS):
            keep = rank2_ref[hd].reshape(slabs, pack, tm) < row_tile(n1_ref, hd, l)
            term = jnp.where(keep, p2_ref[hd].reshape(slabs, pack, tm), zero) * row_tile(p1_ref, hd, l)
            w = term if w is None else w + term
        y_ref[es, :] = act * w.reshape(PEER_NKEYS, tm)
    acc_ref[...] += _dot(vt_ref[...], y_ref[...])

    @pl.when(e == pl.num_programs(1) - 1)
    def _():
        ffn = jnp.transpose(acc_ref[...])
        out_ref[...] = _layer_norm(ALPHA * h_ref[...] + ffn, g_ref[...], b_ref[...])


def _peer(xt, u, vt, rank2, p2, n1, p1, h, g, b, tm, te):
    t, d = h.shape
    rows = te // PEER_NKEYS
    kern = functools.partial(_peer_kernel, rows=rows)
    route_blk = pl.BlockSpec((PEER_HEADS, PEER_NKEYS, tm), lambda i, e: (0, 0, i))
    row_blk = pl.BlockSpec((PEER_HEADS, rows, tm), lambda i, e: (0, e, i))
    return pl.pallas_call(
        kern,
        grid=(t // tm, PEER_EXPERTS // te),
        in_specs=[
            pl.BlockSpec((d, tm), lambda i, e: (0, i)),
            pl.BlockSpec((te, d), lambda i, e: (e, 0)),
            pl.BlockSpec((d, te), lambda i, e: (0, e)),
            route_blk, route_blk, row_blk, row_blk,
            pl.BlockSpec((tm, d), lambda i, e: (i, 0)),
            pl.BlockSpec((1, d), lambda i, e: (0, 0)),
            pl.BlockSpec((1, d), lambda i, e: (0, 0)),
        ],
        out_specs=pl.BlockSpec((tm, d), lambda i, e: (i, 0)),
        out_shape=jax.ShapeDtypeStruct((t, d), F32),
        scratch_shapes=[pltpu.VMEM((d, tm), F32), pltpu.VMEM((te, tm), BF16)],
        compiler_params=_params(("parallel", "arbitrary")),
        name="peer_dense",
    )(xt, u, vt, rank2, p2, n1, p1, h, g, b)


def _pack_w_in(w):
    q, k, v, r, a, su, sv = jnp.split(
        w, [256, 512, 1024, 1536, 1552, 2064], axis=-1)
    a = jnp.pad(a, ((0, 0), (0, A_PAD - GLA_GATE_RANK)))
    return jnp.concatenate([q, k, v, r, su, sv, a], axis=-1).astype(BF16)


def _forward(x, ln_in_g, ln_in_b, w_in, w_gate_up, b_gate, gla_norm_g, sgu_ln_g, sgu_ln_b,
             sgu_w, sgu_b, w_out, ln1_g, ln1_b, peer_wq, peer_k1, peer_k2, peer_u, peer_v,
             ln2_g, ln2_b, *, tm_proj, ts_gla, tm_mix, tm_route, tm_peer, te_peer):
    batch, seq, d = x.shape
    t = batch * seq
    row = lambda p: p.reshape(1, -1)
    h = _ln(x.reshape(t, d), ln_in_g, ln_in_b, tm_proj)
    for l in range(DEPTH):
        p = _in_proj(h, _pack_w_in(w_in[l]), tm_proj)
        wg = jnp.pad(w_gate_up[l], ((0, A_PAD - GLA_GATE_RANK), (0, 0)))
        o = _gla(p.reshape(batch, seq, P_COLS), wg, row(b_gate[l]), row(gla_norm_g[l]),
                 batch, seq, ts_gla)
        bs_full = jnp.broadcast_to(sgu_b[l][:, :, None],
                                   (SGU_GROUPS, SGU_WINDOW, SGU_GROUP_DIM))
        h = _mix_out(p, o.reshape(t, GLA_WIDTH), h, row(sgu_ln_g[l]), row(sgu_ln_b[l]),
                     sgu_w[l], bs_full, w_out[l].astype(BF16), row(ln1_g[l]), row(ln1_b[l]),
                     tm_mix)
        wq = peer_wq[l].reshape(d, PEER_HEADS, PEER_DQ).transpose(1, 0, 2).astype(BF16)
        xt, rank2, p2, n1, p1 = _route(h, wq, peer_k1[l].astype(BF16),
                                       peer_k2[l].astype(BF16), tm_route)
        h = _peer(xt, peer_u[l].astype(BF16), jnp.transpose(peer_v[l]).astype(BF16),
                  rank2, p2, n1, p1, h, row(ln2_g[l]), row(ln2_b[l]), tm_peer, te_peer)
    return h.reshape(batch, seq, d)


def kernel(x, ln_in_g, ln_in_b, w_in, w_gate_up, b_gate, gla_norm_g, sgu_ln_g, sgu_ln_b, sgu_w,
           sgu_b, w_out, ln1_g, ln1_b, peer_wq, peer_k1, peer_k2, peer_u, peer_v, ln2_g, ln2_b):
    return _forward(x, ln_in_g, ln_in_b, w_in, w_gate_up, b_gate, gla_norm_g, sgu_ln_g,
                    sgu_ln_b, sgu_w, sgu_b, w_out, ln1_g, ln1_b, peer_wq, peer_k1, peer_k2,
                    peer_u, peer_v, ln2_g, ln2_b,
                    tm_proj=512, ts_gla=256, tm_mix=512, tm_route=512, tm_peer=512,
                    te_peer=1024)
```

```python
import functools
import math

import jax
import jax.numpy as jnp
from jax import lax
from jax.experimental import pallas as pl
from jax.experimental.pallas import tpu as pltpu

F32 = jnp.float32
BF16 = jnp.bfloat16

D_MODEL = 1024
DEPTH = 2
CHUNK = 64
GLA_HEADS = 4
GLA_DK = 64
GLA_DV = 128
GLA_WIDTH = GLA_HEADS * GLA_DV
GLA_QK = GLA_HEADS * GLA_DK
GLA_GATE_RANK = 16
GLA_GATE_NORMALIZER = 16.0
SGU_WIDTH = 512
SGU_GROUPS = 4
SGU_GROUP_DIM = 128
SGU_WINDOW = 128
PEER_HEADS = 8
PEER_NKEYS = 128
PEER_EXPERTS = PEER_NKEYS * PEER_NKEYS
PEER_TOPK = 16
PEER_DQ = 256
PEER_DQ_HALF = 128
LN_EPS = 1e-5
ALPHA = (2.0 * DEPTH) ** 0.25

LANES = 128
SUBLANES = 8

COL_Q = 0
COL_K = COL_Q + GLA_QK
COL_V = COL_K + GLA_QK
COL_R = COL_V + GLA_WIDTH
COL_SU = COL_R + GLA_WIDTH
COL_SV = COL_SU + SGU_WIDTH
COL_A = COL_SV + SGU_WIDTH
A_PAD = LANES
P_COLS = COL_A + A_PAD

SUB = 16
EXP_CLAMP = 80.0
NEG_INF = float("-inf")

VMEM_LIMIT = 56 * 1024 * 1024


def _params(sem, flags=None):
    return pltpu.CompilerParams(dimension_semantics=sem, vmem_limit_bytes=VMEM_LIMIT, flags=flags)


def _layer_norm(x, g, b):
    mu = jnp.mean(x, axis=-1, keepdims=True)
    xc = x - mu
    var = jnp.mean(xc * xc, axis=-1, keepdims=True)
    return xc * lax.rsqrt(var + LN_EPS) * g + b


def _gelu(x):
    return 0.5 * x * (1.0 + lax.erf(x * (1.0 / math.sqrt(2.0))))


def _dot(a, b):
    return jnp.dot(a, b, preferred_element_type=F32)


def _dot_nt(a, b):
    return lax.dot_general(a, b, (((1,), (1,)), ((), ())), preferred_element_type=F32)


def _dot_tn(a, b):
    return lax.dot_general(a, b, (((0,), (0,)), ((), ())), preferred_element_type=F32)


def _in_proj_ln_kernel(x_ref, g_ref, b_ref, w_ref, h_ref, o_ref):
    h = _layer_norm(x_ref[...], g_ref[...], b_ref[...])
    h_ref[...] = h
    o_ref[...] = _dot(h.astype(BF16), w_ref[...])


def _in_proj_ln(x, g, b, w, tm):
    t, d = x.shape
    n = w.shape[1]
    return pl.pallas_call(
        _in_proj_ln_kernel,
        grid=(t // tm,),
        in_specs=[pl.BlockSpec((tm, d), lambda i: (i, 0)),
                  pl.BlockSpec((1, d), lambda i: (0, 0)),
                  pl.BlockSpec((1, d), lambda i: (0, 0)),
                  pl.BlockSpec((d, n), lambda i: (0, 0))],
        out_specs=[pl.BlockSpec((tm, d), lambda i: (i, 0)),
                   pl.BlockSpec((tm, n), lambda i: (i, 0))],
        out_shape=[jax.ShapeDtypeStruct((t, d), F32), jax.ShapeDtypeStruct((t, n), F32)],
        compiler_params=_params(("parallel",)),
        name="in_proj_ln",
    )(x, g.reshape(1, d), b.reshape(1, d), w)


def _in_proj_kernel(h_ref, w_ref, o_ref):
    o_ref[...] = _dot(h_ref[...].astype(BF16), w_ref[...])


def _in_proj(h, w, tm):
    t, d = h.shape
    n = w.shape[1]
    return pl.pallas_call(
        _in_proj_kernel,
        grid=(t // tm,),
        in_specs=[pl.BlockSpec((tm, d), lambda i: (i, 0)),
                  pl.BlockSpec((d, n), lambda i: (0, 0))],
        out_specs=pl.BlockSpec((tm, n), lambda i: (i, 0)),
        out_shape=jax.ShapeDtypeStruct((t, n), F32),
        compiler_params=_params(("parallel",)),
        name="in_proj",
    )(h, w)


def _gla_kernel(q_ref, k_ref, v_ref, r_ref, a_ref, wg_ref, bg_ref, ng_ref, o_ref, st_ref,
                *, chunks, nb):
    @pl.when(pl.program_id(1) == 0)
    def _():
        st_ref[...] = jnp.zeros_like(st_ref)

    ts = chunks * CHUNK
    row = lax.broadcasted_iota(jnp.int32, (CHUNK, CHUNK), 0)
    col = lax.broadcasted_iota(jnp.int32, (CHUNK, CHUNK), 1)
    causal = col <= row
    trow = lax.broadcasted_iota(jnp.int32, (ts, ts), 0)
    tcol = lax.broadcasted_iota(jnp.int32, (ts, ts), 1)
    tril = jnp.logical_and(tcol <= trow, tcol // CHUNK == trow // CHUNK).astype(F32)
    lane_head = lax.broadcasted_iota(jnp.int32, (1, LANES), 1) // GLA_DK
    ng = ng_ref[...]
    groups = GLA_QK // LANES
    per_group = LANES // GLA_DK
    items = [(bi, c) for bi in range(nb) for c in range(chunks)]

    bc = []
    for bi in range(nb):
        z = jnp.dot(a_ref[bi], wg_ref[...], preferred_element_type=F32,
                    precision=lax.Precision.HIGHEST) + bg_ref[...]
        logg = (jnp.minimum(z, 0.0) - jnp.log1p(jnp.exp(-jnp.abs(z)))) * (1.0 / GLA_GATE_NORMALIZER)
        bc.append(jnp.dot(tril, logg, preferred_element_type=F32, precision=lax.Precision.HIGHEST))

    scores = {}
    for bi, c in items:
        rows = slice(c * CHUNK, (c + 1) * CHUNK)
        for grp in range(groups):
            ls = slice(grp * LANES, (grp + 1) * LANES)
            qg = q_ref[bi, rows, ls] * (GLA_DK ** -0.5)
            kg = k_ref[bi, rows, ls]
            bg = bc[bi][rows, ls]
            parts = []
            for i in range(CHUNK // SUB):
                lo, hi = i * SUB, (i + 1) * SUB
                ref = bg[lo:lo + 1, :]
                qi = qg[lo:hi, :] * jnp.exp(bg[lo:hi, :] - ref)
                ki = (kg * jnp.exp(jnp.minimum(ref - bg, EXP_CLAMP))).astype(BF16)
                q2 = jnp.concatenate([jnp.where(lane_head == 0, qi, 0.0),
                                      jnp.where(lane_head == 1, qi, 0.0)], axis=0).astype(BF16)
                parts.append(_dot_nt(q2, ki))
            scores[bi, c, grp] = parts

    intra = {}
    for bi, c in items:
        rows = slice(c * CHUNK, (c + 1) * CHUNK)
        for h in range(GLA_HEADS):
            grp, sub = divmod(h, per_group)
            att = jnp.concatenate([p[sub * SUB:(sub + 1) * SUB, :] for p in scores[bi, c, grp]],
                                  axis=0)
            a_h = jnp.where(causal, att, 0.0).astype(BF16)
            vh_b = v_ref[bi, rows, h * GLA_DV:(h + 1) * GLA_DV].astype(BF16)
            intra[bi, c, h] = _dot(a_h, vh_b)

    for c in range(chunks):
        rows = slice(c * CHUNK, (c + 1) * CHUNK)
        for bi in range(nb):
            for grp in range(groups):
                ls = slice(grp * LANES, (grp + 1) * LANES)
                qg = q_ref[bi, rows, ls] * (GLA_DK ** -0.5)
                kg = k_ref[bi, rows, ls]
                bg = bc[bi][rows, ls]
                b_last = bg[CHUNK - 1:CHUNK, :]
                q_inter = (qg * jnp.exp(bg)).astype(BF16)
                kd = kg * jnp.exp(b_last - bg)
                decay = jnp.exp(b_last)
                for sub in range(per_group):
                    h = grp * per_group + sub
                    vs = slice(h * GLA_DV, (h + 1) * GLA_DV)
                    vh_b = v_ref[bi, rows, vs].astype(BF16)
                    state_t = st_ref[bi, h]
                    o = _dot_nt(q_inter, state_t.astype(BF16)) + intra[bi, c, h]
                    kd_h = jnp.where(lane_head == sub, kd, 0.0).astype(BF16)
                    st_ref[bi, h] = state_t * decay + _dot_tn(vh_b, kd_h)
                    ms = jnp.mean(o * o, axis=-1, keepdims=True)
                    on = o * lax.rsqrt(ms + LN_EPS) * ng
                    rh = r_ref[bi, rows, vs]
                    o_ref[bi, rows, vs] = on * (rh * (1.0 / (1.0 + jnp.exp(-rh))))


def _gla(p, wg, bg, ng, batch, seq, ts, nb):
    kern = functools.partial(_gla_kernel, chunks=ts // CHUNK, nb=nb)
    return pl.pallas_call(
        kern,
        grid=(batch // nb, seq // ts),
        in_specs=[
            pl.BlockSpec((nb, ts, GLA_QK), lambda b, s: (b, s, COL_Q // GLA_QK)),
            pl.BlockSpec((nb, ts, GLA_QK), lambda b, s: (b, s, COL_K // GLA_QK)),
            pl.BlockSpec((nb, ts, GLA_WIDTH), lambda b, s: (b, s, COL_V // GLA_WIDTH)),
            pl.BlockSpec((nb, ts, GLA_WIDTH), lambda b, s: (b, s, COL_R // GLA_WIDTH)),
            pl.BlockSpec((nb, ts, A_PAD), lambda b, s: (b, s, COL_A // A_PAD)),
            pl.BlockSpec((A_PAD, GLA_QK), lambda b, s: (0, 0)),
            pl.BlockSpec((1, GLA_QK), lambda b, s: (0, 0)),
            pl.BlockSpec((1, GLA_DV), lambda b, s: (0, 0)),
        ],
        out_specs=pl.BlockSpec((nb, ts, GLA_WIDTH), lambda b, s: (b, s, 0)),
        out_shape=jax.ShapeDtypeStruct((batch, seq, GLA_WIDTH), F32),
        scratch_shapes=[pltpu.VMEM((nb, GLA_HEADS, GLA_DV, LANES), F32)],
        compiler_params=_params(("parallel", "arbitrary")),
        name="gla",
    )(p, p, p, p, p, wg, bg, ng)


def _mix_out_kernel(su_ref, sv_ref, o_ref, h_ref, lg_ref, lb_ref, ws_ref, bs_ref, wo_ref,
                    g1_ref, b1_ref, out_ref, gate_ref, *, windows):
    u = _gelu(su_ref[...])
    v = _layer_norm(_gelu(sv_ref[...]), lg_ref[...], lb_ref[...])
    blk_r = lax.broadcasted_iota(jnp.int32, (SGU_WINDOW, SGU_WINDOW), 0) // CHUNK
    blk_c = lax.broadcasted_iota(jnp.int32, (SGU_WINDOW, SGU_WINDOW), 1) // CHUNK
    keep = blk_c <= blk_r
    for g in range(SGU_GROUPS):
        cs = slice(g * SGU_GROUP_DIM, (g + 1) * SGU_GROUP_DIM)
        w = jnp.where(keep, ws_ref[g], 0.0).astype(BF16)
        bias = bs_ref[g]
        for n in range(windows):
            rs = slice(n * SGU_WINDOW, (n + 1) * SGU_WINDOW)
            sv = _dot(w, v[rs, cs].astype(BF16)) + bias
            gate_ref[rs, cs] = u[rs, cs] * sv
    mix = _dot(o_ref[...].astype(BF16), wo_ref[0:GLA_WIDTH, :])
    mix = mix + _dot(gate_ref[...].astype(BF16), wo_ref[GLA_WIDTH:, :])
    out_ref[...] = _layer_norm(ALPHA * h_ref[...] + mix, g1_ref[...], b1_ref[...])


def _mix_out(p, o, h, lg, lb, ws, bs_full, wo, g1, b1, tm):
    t, d = h.shape
    kern = functools.partial(_mix_out_kernel, windows=tm // SGU_WINDOW)
    const2 = lambda i: (0, 0)
    const3 = lambda i: (0, 0, 0)
    return pl.pallas_call(
        kern,
        grid=(t // tm,),
        in_specs=[
            pl.BlockSpec((tm, SGU_WIDTH), lambda i: (i, COL_SU // SGU_WIDTH)),
            pl.BlockSpec((tm, SGU_WIDTH), lambda i: (i, COL_SV // SGU_WIDTH)),
            pl.BlockSpec((tm, GLA_WIDTH), lambda i: (i, 0)),
            pl.BlockSpec((tm, d), lambda i: (i, 0)),
            pl.BlockSpec((1, SGU_WIDTH), const2),
            pl.BlockSpec((1, SGU_WIDTH), const2),
            pl.BlockSpec((SGU_GROUPS, SGU_WINDOW, SGU_WINDOW), const3),
            pl.BlockSpec((SGU_GROUPS, SGU_WINDOW, SGU_GROUP_DIM), const3),
            pl.BlockSpec((d, d), const2),
            pl.BlockSpec((1, d), const2),
            pl.BlockSpec((1, d), const2),
        ],
        out_specs=pl.BlockSpec((tm, d), lambda i: (i, 0)),
        out_shape=jax.ShapeDtypeStruct((t, d), F32),
        scratch_shapes=[pltpu.VMEM((tm, SGU_WIDTH), F32)],
        compiler_params=_params(("parallel",)),
        name="mix_out",
    )(p, p, o, h, lg, lb, ws, bs_full, wo, g1, b1)


def _sort_network(n):
    pairs = []

    def merge(lo, hi, r):
        step = r * 2
        if step < hi - lo:
            merge(lo, hi, step)
            merge(lo + r, hi, step)
            pairs.extend((i, i + r) for i in range(lo + r, hi - r, step))
        else:
            pairs.append((lo, lo + r))

    def sort(lo, hi):
        if hi - lo >= 1:
            mid = lo + (hi - lo) // 2
            sort(lo, mid)
            sort(mid + 1, hi)
            merge(lo, hi, 1)

    sort(0, n - 1)
    return pairs


def _ce(x, y):
    if y is None:
        return x, None
    if x is None:
        return y, None
    return jnp.maximum(x, y), jnp.minimum(x, y)


def _merge_top(a, b):
    n = len(a)
    v = [_ce(a[k], b[n - 1 - k])[0] for k in range(n)]
    d = n // 2
    while d >= 1:
        for i in range(n):
            if i & d == 0:
                v[i], v[i + d] = _ce(v[i], v[i + d])
        d //= 2
    return v


def _top_sorted(tiles):
    v = list(tiles)
    for i, j in _sort_network(len(v)):
        v[i], v[j] = _ce(v[i], v[j])
    shift = SUBLANES // 2
    while shift >= 1:
        v = _merge_top(v, [pltpu.roll(x, shift, axis=0) for x in v])
        shift //= 2
    return v


def _pad(vals):
    return list(vals) + [None] * (PEER_TOPK - len(vals))


def _route_kernel(h_ref, wq_ref, k1_ref, k2_ref, xt_ref, rank2_ref, p2_ref, n1_ref, p1_ref,
                  xb_ref, s1_ref, s2_ref):
    tm = h_ref.shape[0]
    x = h_ref[...]
    xt_ref[...] = jnp.transpose(x).astype(BF16)
    xb_ref[...] = x.astype(BF16)
    ntile = PEER_NKEYS // SUBLANES

    def head_body(hd, carry):
        q = _dot(xb_ref[...], wq_ref[hd])
        s1_ref[...] = _dot_nt(k1_ref[...], q[:, :PEER_DQ_HALF].astype(BF16))
        s2_ref[...] = _dot_nt(k2_ref[...], q[:, PEER_DQ_HALF:].astype(BF16))
        for c in range(tm // LANES):
            ls = slice(c * LANES, (c + 1) * LANES)
            tile = lambda ref, k: ref[k * SUBLANES:(k + 1) * SUBLANES, ls]
            a = _top_sorted([tile(s1_ref, k) for k in range(ntile)])
            b = _top_sorted([tile(s2_ref, k) for k in range(ntile)])
            row = [_pad([a[i] + b[j] for j in range(PEER_TOPK // (i + 1))])
                   for i in range(SUBLANES)]
            tail = _pad([a[i] + b[0] for i in range(SUBLANES, PEER_TOPK)])
            small = _merge_top(_merge_top(row[4], row[5]), _merge_top(row[6], row[7]))
            mid = _merge_top(_merge_top(row[2], row[3]), small)
            top = _merge_top(row[0], _merge_top(_merge_top(row[1], tail), mid))
            thr = top[PEER_TOPK - 1]
            z = None
            for r in range(PEER_TOPK):
                e = jnp.exp(top[r] - top[0])
                z = e if z is None else z + e
            inv_z = 1.0 / z
            cut = []
            for j in range(PEER_TOPK):
                cj = jnp.full_like(thr, float("inf"))
                for i in range(min(PEER_TOPK // (j + 1), SUBLANES)):
                    cj = jnp.where(row[i][j] >= thr, a[i], cj)
                if j == 0:
                    for i in range(SUBLANES, PEER_TOPK):
                        cj = jnp.where(tail[i - SUBLANES] >= thr, a[i], cj)
                cut.append(cj)
            for kp in range(ntile // 2):
                r2, pp2 = [], []
                for k in (2 * kp, 2 * kp + 1):
                    rows = slice(k * SUBLANES, (k + 1) * SUBLANES)
                    s1k = tile(s1_ref, k)
                    n1 = jnp.zeros_like(s1k)
                    for j in range(PEER_TOPK):
                        n1 = jnp.where(s1k >= cut[j], float(j + 1), n1)
                    n1_ref[hd, rows, ls] = n1
                    p1_ref[hd, rows, ls] = jnp.exp(s1k - a[0]) * inv_z
                    s2k = tile(s2_ref, k)
                    rk = jnp.full_like(s2k, float(PEER_TOPK))
                    for r in range(PEER_TOPK - 1, -1, -1):
                        rk = jnp.where(s2k >= b[r], float(r), rk)
                    r2.append(rk)
                    pp2.append(jnp.exp(s2k - b[0]))
                rows2 = slice(kp * 2 * SUBLANES, (kp + 1) * 2 * SUBLANES)
                rank2_ref[hd, rows2, ls] = jnp.concatenate(r2, axis=0).astype(BF16)
                p2_ref[hd, rows2, ls] = jnp.concatenate(pp2, axis=0).astype(BF16)
        return carry

    lax.fori_loop(0, PEER_HEADS, head_body, 0)


def _route(h, wq, k1, k2, tm):
    t, d = h.shape
    hk = (PEER_HEADS, PEER_NKEYS, t)
    blk = pl.BlockSpec((PEER_HEADS, PEER_NKEYS, tm), lambda i: (0, 0, i))
    return pl.pallas_call(
        _route_kernel,
        grid=(t // tm,),
        in_specs=[
            pl.BlockSpec((tm, d), lambda i: (i, 0)),
            pl.BlockSpec((PEER_HEADS, d, PEER_DQ), lambda i: (0, 0, 0)),
            pl.BlockSpec((PEER_NKEYS, PEER_DQ_HALF), lambda i: (0, 0)),
            pl.BlockSpec((PEER_NKEYS, PEER_DQ_HALF), lambda i: (0, 0)),
        ],
        out_specs=[pl.BlockSpec((d, tm), lambda i: (0, i)), blk, blk, blk, blk],
        out_shape=[
            jax.ShapeDtypeStruct((d, t), BF16),
            jax.ShapeDtypeStruct(hk, BF16),
            jax.ShapeDtypeStruct(hk, BF16),
            jax.ShapeDtypeStruct(hk, F32),
            jax.ShapeDtypeStruct(hk, F32),
        ],
        scratch_shapes=[pltpu.VMEM((tm, d), BF16),
                        pltpu.VMEM((PEER_NKEYS, tm), F32),
                        pltpu.VMEM((PEER_NKEYS, tm), F32)],
        compiler_params=_params(("parallel",)),
        name="peer_route",
    )(h, wq, k1, k2)


def _peer_kernel(xt_ref, u_ref, v_ref, rank2_ref, p2_ref, n1_ref, p1_ref, h_ref, g_ref, b_ref,
                 out_ref, acc_ref, y_ref, *, rows):
    e = pl.program_id(1)

    @pl.when(e == 0)
    def _():
        acc_ref[...] = jnp.zeros_like(acc_ref)

    xt = xt_ref[...]
    tm = xt.shape[1]
    pack = 2 * SUBLANES
    slabs = PEER_NKEYS // pack
    zero = jnp.zeros((), BF16)

    def row_tile(ref, hd, l):
        r8 = jnp.broadcast_to(ref[hd, l:l + 1, :], (SUBLANES, tm))
        return jnp.concatenate([r8, r8], axis=0).astype(BF16)[None]

    for l in range(rows):
        es = slice(l * PEER_NKEYS, (l + 1) * PEER_NKEYS)
        act = _gelu(_dot(u_ref[es, :], xt).astype(BF16))
        w = None
        for hd in range(PEER_HEADS):
            keep = rank2_ref[hd].reshape(slabs, pack, tm) < row_tile(n1_ref, hd, l)
            term = jnp.where(keep, p2_ref[hd].reshape(slabs, pack, tm), zero) * row_tile(p1_ref, hd, l)
            w = term if w is None else w + term
        y_ref[es, :] = act * w.reshape(PEER_NKEYS, tm)
    acc_ref[...] += _dot_tn(v_ref[...], y_ref[...])

    @pl.when(e == pl.num_programs(1) - 1)
    def _():
        ffn = jnp.transpose(acc_ref[...])
        out_ref[...] = _layer_norm(ALPHA * h_ref[...] + ffn, g_ref[...], b_ref[...])


def _peer(xt, u, v, rank2, p2, n1, p1, h, g, b, tm, te):
    t, d = h.shape
    rows = te // PEER_NKEYS
    kern = functools.partial(_peer_kernel, rows=rows)
    route_blk = pl.BlockSpec((PEER_HEADS, PEER_NKEYS, tm), lambda i, e: (0, 0, i))
    row_blk = pl.BlockSpec((PEER_HEADS, rows, tm), lambda i, e: (0, e, i))
    return pl.pallas_call(
        kern,
        grid=(t // tm, PEER_EXPERTS // te),
        in_specs=[
            pl.BlockSpec((d, tm), lambda i, e: (0, i)),
            pl.BlockSpec((te, d), lambda i, e: (e, 0)),
            pl.BlockSpec((te, d), lambda i, e: (e, 0)),
            route_blk, route_blk, row_blk, row_blk,
            pl.BlockSpec((tm, d), lambda i, e: (i, 0)),
            pl.BlockSpec((1, d), lambda i, e: (0, 0)),
            pl.BlockSpec((1, d), lambda i, e: (0, 0)),
        ],
        out_specs=pl.BlockSpec((tm, d), lambda i, e: (i, 0)),
        out_shape=jax.ShapeDtypeStruct((t, d), F32),
        scratch_shapes=[pltpu.VMEM((d, tm), F32), pltpu.VMEM((te, tm), BF16)],
        compiler_params=_params(("parallel", "arbitrary")),
        name="peer_dense",
    )(xt, u, v, rank2, p2, n1, p1, h, g, b)


def _pack_w_in(w):
    q, k, v, r, a, su, sv = jnp.split(
        w, [256, 512, 1024, 1536, 1552, 2064], axis=-1)
    a = jnp.pad(a, ((0, 0), (0, A_PAD - GLA_GATE_RANK)))
    return jnp.concatenate([q, k, v, r, su, sv, a], axis=-1).astype(BF16)


def _forward(x, ln_in_g, ln_in_b, w_in, w_gate_up, b_gate, gla_norm_g, sgu_ln_g, sgu_ln_b,
             sgu_w, sgu_b, w_out, ln1_g, ln1_b, peer_wq, peer_k1, peer_k2, peer_u, peer_v,
             ln2_g, ln2_b, *, tm_proj, ts_gla, nb_gla, tm_mix, tm_route, tm_peer, te_peer):
    batch, seq, d = x.shape
    t = batch * seq
    row = lambda p: p.reshape(1, -1)
    h = None
    for l in range(DEPTH):
        if l == 0:
            h, p = _in_proj_ln(x.reshape(t, d), ln_in_g, ln_in_b, _pack_w_in(w_in[l]), tm_proj)
        else:
            p = _in_proj(h, _pack_w_in(w_in[l]), tm_proj)
        wg = jnp.pad(w_gate_up[l], ((0, A_PAD - GLA_GATE_RANK), (0, 0)))
        o = _gla(p.reshape(batch, seq, P_COLS), wg, row(b_gate[l]), row(gla_norm_g[l]),
                 batch, seq, ts_gla, nb_gla)
        bs_full = jnp.broadcast_to(sgu_b[l][:, :, None],
                                   (SGU_GROUPS, SGU_WINDOW, SGU_GROUP_DIM))
        h = _mix_out(p, o.reshape(t, GLA_WIDTH), h, row(sgu_ln_g[l]), row(sgu_ln_b[l]),
                     sgu_w[l], bs_full, w_out[l].astype(BF16), row(ln1_g[l]), row(ln1_b[l]),
                     tm_mix)
        wq = peer_wq[l].reshape(d, PEER_HEADS, PEER_DQ).transpose(1, 0, 2).astype(BF16)
        xt, rank2, p2, n1, p1 = _route(h, wq, peer_k1[l].astype(BF16),
                                       peer_k2[l].astype(BF16), tm_route)
        h = _peer(xt, peer_u[l].astype(BF16), peer_v[l].astype(BF16),
                  rank2, p2, n1, p1, h, row(ln2_g[l]), row(ln2_b[l]), tm_peer, te_peer)
    return h.reshape(batch, seq, d)


def kernel(x, ln_in_g, ln_in_b, w_in, w_gate_up, b_gate, gla_norm_g, sgu_ln_g, sgu_ln_b, sgu_w,
           sgu_b, w_out, ln1_g, ln1_b, peer_wq, peer_k1, peer_k2, peer_u, peer_v, ln2_g, ln2_b):
    return _forward(x, ln_in_g, ln_in_b, w_in, w_gate_up, b_gate, gla_norm_g, sgu_ln_g,
                    sgu_ln_b, sgu_w, sgu_b, w_out, ln1_g, ln1_b, peer_wq, peer_k1, peer_k2,
                    peer_u, peer_v, ln2_g, ln2_b,
                    tm_proj=512, ts_gla=256, nb_gla=4, tm_mix=512, tm_route=512, tm_peer=512,
                    te_peer=1024)
```

```python
import functools
import math

import jax
import jax.numpy as jnp
from jax import lax
from jax.experimental import pallas as pl
from jax.experimental.pallas import tpu as pltpu

F32 = jnp.float32
BF16 = jnp.bfloat16

D_MODEL = 1024
DEPTH = 2
CHUNK = 64
GLA_HEADS = 4
GLA_DK = 64
GLA_DV = 128
GLA_WIDTH = GLA_HEADS * GLA_DV
GLA_QK = GLA_HEADS * GLA_DK
GLA_GATE_RANK = 16
GLA_GATE_NORMALIZER = 16.0
SGU_WIDTH = 512
SGU_GROUPS = 4
SGU_GROUP_DIM = 128
SGU_WINDOW = 128
PEER_HEADS = 8
PEER_NKEYS = 128
PEER_EXPERTS = PEER_NKEYS * PEER_NKEYS
PEER_TOPK = 16
PEER_DQ = 256
PEER_DQ_HALF = 128
LN_EPS = 1e-5
ALPHA = (2.0 * DEPTH) ** 0.25

LANES = 128
SUBLANES = 8

COL_Q = 0
COL_K = COL_Q + GLA_QK
COL_V = COL_K + GLA_QK
COL_R = COL_V + GLA_WIDTH
COL_SU = COL_R + GLA_WIDTH
COL_SV = COL_SU + SGU_WIDTH
COL_A = COL_SV + SGU_WIDTH
A_PAD = LANES
P_COLS = COL_A + A_PAD

SUB = 16
EXP_CLAMP = 80.0
NEG_INF = float("-inf")

VMEM_LIMIT = 56 * 1024 * 1024


def _params(sem, flags=None):
    return pltpu.CompilerParams(dimension_semantics=sem, vmem_limit_bytes=VMEM_LIMIT, flags=flags)


def _layer_norm(x, g, b):
    mu = jnp.mean(x, axis=-1, keepdims=True)
    xc = x - mu
    var = jnp.mean(xc * xc, axis=-1, keepdims=True)
    return xc * lax.rsqrt(var + LN_EPS) * g + b


def _gelu(x):
    return 0.5 * x * (1.0 + lax.erf(x * (1.0 / math.sqrt(2.0))))


def _dot(a, b):
    return jnp.dot(a, b, preferred_element_type=F32)


def _dot_nt(a, b):
    return lax.dot_general(a, b, (((1,), (1,)), ((), ())), preferred_element_type=F32)


def _dot_tn(a, b):
    return lax.dot_general(a, b, (((0,), (0,)), ((), ())), preferred_element_type=F32)


def _in_proj_ln_kernel(x_ref, g_ref, b_ref, w_ref, h_ref, o_ref):
    h = _layer_norm(x_ref[...], g_ref[...], b_ref[...])
    h_ref[...] = h
    o_ref[...] = _dot(h.astype(BF16), w_ref[...])


def _in_proj_ln(x, g, b, w, tm):
    t, d = x.shape
    n = w.shape[1]
    return pl.pallas_call(
        _in_proj_ln_kernel,
        grid=(t // tm,),
        in_specs=[pl.BlockSpec((tm, d), lambda i: (i, 0)),
                  pl.BlockSpec((1, d), lambda i: (0, 0)),
                  pl.BlockSpec((1, d), lambda i: (0, 0)),
                  pl.BlockSpec((d, n), lambda i: (0, 0))],
        out_specs=[pl.BlockSpec((tm, d), lambda i: (i, 0)),
                   pl.BlockSpec((tm, n), lambda i: (i, 0))],
        out_shape=[jax.ShapeDtypeStruct((t, d), F32), jax.ShapeDtypeStruct((t, n), F32)],
        compiler_params=_params(("parallel",)),
        name="in_proj_ln",
    )(x, g.reshape(1, d), b.reshape(1, d), w)


def _in_proj_kernel(h_ref, w_ref, o_ref):
    o_ref[...] = _dot(h_ref[...].astype(BF16), w_ref[...])


def _in_proj(h, w, tm):
    t, d = h.shape
    n = w.shape[1]
    return pl.pallas_call(
        _in_proj_kernel,
        grid=(t // tm,),
        in_specs=[pl.BlockSpec((tm, d), lambda i: (i, 0)),
                  pl.BlockSpec((d, n), lambda i: (0, 0))],
        out_specs=pl.BlockSpec((tm, n), lambda i: (i, 0)),
        out_shape=jax.ShapeDtypeStruct((t, n), F32),
        compiler_params=_params(("parallel",)),
        name="in_proj",
    )(h, w)


def _gla_kernel(q_ref, k_ref, v_ref, r_ref, a_ref, wg_ref, bg_ref, ng_ref, o_ref, st_ref,
                *, chunks, nb):
    @pl.when(pl.program_id(1) == 0)
    def _():
        st_ref[...] = jnp.zeros_like(st_ref)

    ts = chunks * CHUNK
    row = lax.broadcasted_iota(jnp.int32, (CHUNK, CHUNK), 0)
    col = lax.broadcasted_iota(jnp.int32, (CHUNK, CHUNK), 1)
    causal = col <= row
    trow = lax.broadcasted_iota(jnp.int32, (ts, ts), 0)
    tcol = lax.broadcasted_iota(jnp.int32, (ts, ts), 1)
    tril = jnp.logical_and(tcol <= trow, tcol // CHUNK == trow // CHUNK).astype(F32)
    lane_head = lax.broadcasted_iota(jnp.int32, (1, LANES), 1) // GLA_DK
    ng = ng_ref[...]
    groups = GLA_QK // LANES
    per_group = LANES // GLA_DK
    items = [(bi, c) for bi in range(nb) for c in range(chunks)]

    bc = []
    for bi in range(nb):
        z = jnp.dot(a_ref[bi], wg_ref[...], preferred_element_type=F32,
                    precision=lax.Precision.HIGHEST) + bg_ref[...]
        logg = (jnp.minimum(z, 0.0) - jnp.log1p(jnp.exp(-jnp.abs(z)))) * (1.0 / GLA_GATE_NORMALIZER)
        bc.append(jnp.dot(tril, logg, preferred_element_type=F32, precision=lax.Precision.HIGHEST))

    scores = {}
    for bi, c in items:
        rows = slice(c * CHUNK, (c + 1) * CHUNK)
        for grp in range(groups):
            ls = slice(grp * LANES, (grp + 1) * LANES)
            qg = q_ref[bi, rows, ls] * (GLA_DK ** -0.5)
            kg = k_ref[bi, rows, ls]
            bg = bc[bi][rows, ls]
            parts = []
            for i in range(CHUNK // SUB):
                lo, hi = i * SUB, (i + 1) * SUB
                ref = bg[lo:lo + 1, :]
                qi = qg[lo:hi, :] * jnp.exp(bg[lo:hi, :] - ref)
                ki = (kg * jnp.exp(jnp.minimum(ref - bg, EXP_CLAMP))).astype(BF16)
                q2 = jnp.concatenate([jnp.where(lane_head == 0, qi, 0.0),
                                      jnp.where(lane_head == 1, qi, 0.0)], axis=0).astype(BF16)
                parts.append(_dot_nt(q2, ki))
            scores[bi, c, grp] = parts

    intra = {}
    for bi, c in items:
        rows = slice(c * CHUNK, (c + 1) * CHUNK)
        for h in range(GLA_HEADS):
            grp, sub = divmod(h, per_group)
            att = jnp.concatenate([p[sub * SUB:(sub + 1) * SUB, :] for p in scores[bi, c, grp]],
                                  axis=0)
            a_h = jnp.where(causal, att, 0.0).astype(BF16)
            vh_b = v_ref[bi, rows, h * GLA_DV:(h + 1) * GLA_DV].astype(BF16)
            intra[bi, c, h] = _dot(a_h, vh_b)

    for c in range(chunks):
        rows = slice(c * CHUNK, (c + 1) * CHUNK)
        for bi in range(nb):
            for grp in range(groups):
                ls = slice(grp * LANES, (grp + 1) * LANES)
                qg = q_ref[bi, rows, ls] * (GLA_DK ** -0.5)
                kg = k_ref[bi, rows, ls]
                bg = bc[bi][rows, ls]
                b_last = bg[CHUNK - 1:CHUNK, :]
                q_inter = (qg * jnp.exp(bg)).astype(BF16)
                kd = kg * jnp.exp(b_last - bg)
                decay = jnp.exp(b_last)
                for sub in range(per_group):
                    h = grp * per_group + sub
                    vs = slice(h * GLA_DV, (h + 1) * GLA_DV)
                    vh_b = v_ref[bi, rows, vs].astype(BF16)
                    state_t = st_ref[bi, h]
                    o = _dot_nt(q_inter, state_t.astype(BF16)) + intra[bi, c, h]
                    kd_h = jnp.where(lane_head == sub, kd, 0.0).astype(BF16)
                    st_ref[bi, h] = state_t * decay + _dot_tn(vh_b, kd_h)
                    ms = jnp.mean(o * o, axis=-1, keepdims=True)
                    on = o * lax.rsqrt(ms + LN_EPS) * ng
                    rh = r_ref[bi, rows, vs]
                    o_ref[bi, rows, vs] = on * (rh * (1.0 / (1.0 + jnp.exp(-rh))))


def _gla(p, wg, bg, ng, batch, seq, ts, nb):
    kern = functools.partial(_gla_kernel, chunks=ts // CHUNK, nb=nb)
    return pl.pallas_call(
        kern,
        grid=(batch // nb, seq // ts),
        in_specs=[
            pl.BlockSpec((nb, ts, GLA_QK), lambda b, s: (b, s, COL_Q // GLA_QK)),
            pl.BlockSpec((nb, ts, GLA_QK), lambda b, s: (b, s, COL_K // GLA_QK)),
            pl.BlockSpec((nb, ts, GLA_WIDTH), lambda b, s: (b, s, COL_V // GLA_WIDTH)),
            pl.BlockSpec((nb, ts, GLA_WIDTH), lambda b, s: (b, s, COL_R // GLA_WIDTH)),
            pl.BlockSpec((nb, ts, A_PAD), lambda b, s: (b, s, COL_A // A_PAD)),
            pl.BlockSpec((A_PAD, GLA_QK), lambda b, s: (0, 0)),
            pl.BlockSpec((1, GLA_QK), lambda b, s: (0, 0)),
            pl.BlockSpec((1, GLA_DV), lambda b, s: (0, 0)),
        ],
        out_specs=pl.BlockSpec((nb, ts, GLA_WIDTH), lambda b, s: (b, s, 0)),
        out_shape=jax.ShapeDtypeStruct((batch, seq, GLA_WIDTH), F32),
        scratch_shapes=[pltpu.VMEM((nb, GLA_HEADS, GLA_DV, LANES), F32)],
        compiler_params=_params(("parallel", "arbitrary")),
        name="gla",
    )(p, p, p, p, p, wg, bg, ng)


def _mix_out_kernel(su_ref, sv_ref, o_ref, h_ref, lg_ref, lb_ref, ws_ref, bs_ref, wo_ref,
                    g1_ref, b1_ref, out_ref, gate_ref, *, windows):
    u = _gelu(su_ref[...])
    v = _layer_norm(_gelu(sv_ref[...]), lg_ref[...], lb_ref[...])
    blk_r = lax.broadcasted_iota(jnp.int32, (SGU_WINDOW, SGU_WINDOW), 0) // CHUNK
    blk_c = lax.broadcasted_iota(jnp.int32, (SGU_WINDOW, SGU_WINDOW), 1) // CHUNK
    keep = blk_c <= blk_r
    for g in range(SGU_GROUPS):
        cs = slice(g * SGU_GROUP_DIM, (g + 1) * SGU_GROUP_DIM)
        w = jnp.where(keep, ws_ref[g], 0.0).astype(BF16)
        bias = bs_ref[g]
        for n in range(windows):
            rs = slice(n * SGU_WINDOW, (n + 1) * SGU_WINDOW)
            sv = _dot(w, v[rs, cs].astype(BF16)) + bias
            gate_ref[rs, cs] = u[rs, cs] * sv
    mix = _dot(o_ref[...].astype(BF16), wo_ref[0:GLA_WIDTH, :])
    mix = mix + _dot(gate_ref[...].astype(BF16), wo_ref[GLA_WIDTH:, :])
    out_ref[...] = _layer_norm(ALPHA * h_ref[...] + mix, g1_ref[...], b1_ref[...])


def _mix_out(p, o, h, lg, lb, ws, bs_full, wo, g1, b1, tm):
    t, d = h.shape
    kern = functools.partial(_mix_out_kernel, windows=tm // SGU_WINDOW)
    const2 = lambda i: (0, 0)
    const3 = lambda i: (0, 0, 0)
    return pl.pallas_call(
        kern,
        grid=(t // tm,),
        in_specs=[
            pl.BlockSpec((tm, SGU_WIDTH), lambda i: (i, COL_SU // SGU_WIDTH)),
            pl.BlockSpec((tm, SGU_WIDTH), lambda i: (i, COL_SV // SGU_WIDTH)),
            pl.BlockSpec((tm, GLA_WIDTH), lambda i: (i, 0)),
            pl.BlockSpec((tm, d), lambda i: (i, 0)),
            pl.BlockSpec((1, SGU_WIDTH), const2),
            pl.BlockSpec((1, SGU_WIDTH), const2),
            pl.BlockSpec((SGU_GROUPS, SGU_WINDOW, SGU_WINDOW), const3),
            pl.BlockSpec((SGU_GROUPS, SGU_WINDOW, SGU_GROUP_DIM), const3),
            pl.BlockSpec((d, d), const2),
            pl.BlockSpec((1, d), const2),
            pl.BlockSpec((1, d), const2),
        ],
        out_specs=pl.BlockSpec((tm, d), lambda i: (i, 0)),
        out_shape=jax.ShapeDtypeStruct((t, d), F32),
        scratch_shapes=[pltpu.VMEM((tm, SGU_WIDTH), F32)],
        compiler_params=_params(("parallel",)),
        name="mix_out",
    )(p, p, o, h, lg, lb, ws, bs_full, wo, g1, b1)


def _sort_network(n):
    pairs = []

    def merge(lo, hi, r):
        step = r * 2
        if step < hi - lo:
            merge(lo, hi, step)
            merge(lo + r, hi, step)
            pairs.extend((i, i + r) for i in range(lo + r, hi - r, step))
        else:
            pairs.append((lo, lo + r))

    def sort(lo, hi):
        if hi - lo >= 1:
            mid = lo + (hi - lo) // 2
            sort(lo, mid)
            sort(mid + 1, hi)
            merge(lo, hi, 1)

    sort(0, n - 1)
    return pairs


def _ce(x, y):
    if y is None:
        return x, None
    if x is None:
        return y, None
    return jnp.maximum(x, y), jnp.minimum(x, y)


def _merge_top(a, b):
    n = len(a)
    v = [_ce(a[k], b[n - 1 - k])[0] for k in range(n)]
    d = n // 2
    while d >= 1:
        for i in range(n):
            if i & d == 0:
                v[i], v[i + d] = _ce(v[i], v[i + d])
        d //= 2
    return v


def _top_sorted(tiles):
    v = list(tiles)
    for i, j in _sort_network(len(v)):
        v[i], v[j] = _ce(v[i], v[j])
    shift = SUBLANES // 2
    while shift >= 1:
        v = _merge_top(v, [pltpu.roll(x, shift, axis=0) for x in v])
        shift //= 2
    return v


def _pad(vals):
    return list(vals) + [None] * (PEER_TOPK - len(vals))


def _route_kernel(h_ref, wq_ref, k1_ref, k2_ref, xt_ref, rank2_ref, p2_ref, n1_ref, p1_ref,
                  xb_ref, s1_ref, s2_ref):
    tm = h_ref.shape[0]
    x = h_ref[...]
    xt_ref[...] = jnp.transpose(x).astype(BF16)
    xb_ref[...] = x.astype(BF16)
    ntile = PEER_NKEYS // SUBLANES

    def head_body(hd, carry):
        q = _dot(xb_ref[...], wq_ref[hd])
        s1_ref[...] = _dot_nt(k1_ref[...], q[:, :PEER_DQ_HALF].astype(BF16))
        s2_ref[...] = _dot_nt(k2_ref[...], q[:, PEER_DQ_HALF:].astype(BF16))
        for c in range(tm // LANES):
            ls = slice(c * LANES, (c + 1) * LANES)
            tile = lambda ref, k: ref[k * SUBLANES:(k + 1) * SUBLANES, ls]
            a = _top_sorted([tile(s1_ref, k) for k in range(ntile)])
            b = _top_sorted([tile(s2_ref, k) for k in range(ntile)])
            row = [_pad([a[i] + b[j] for j in range(PEER_TOPK // (i + 1))])
                   for i in range(SUBLANES)]
            tail = _pad([a[i] + b[0] for i in range(SUBLANES, PEER_TOPK)])
            small = _merge_top(_merge_top(row[4], row[5]), _merge_top(row[6], row[7]))
            mid = _merge_top(_merge_top(row[2], row[3]), small)
            top = _merge_top(row[0], _merge_top(_merge_top(row[1], tail), mid))
            thr = top[PEER_TOPK - 1]
            z = None
            for r in range(PEER_TOPK):
                e = jnp.exp(top[r] - top[0])
                z = e if z is None else z + e
            inv_z = 1.0 / z
            cut = []
            for j in range(PEER_TOPK):
                cj = jnp.full_like(thr, float("inf"))
                for i in range(min(PEER_TOPK // (j + 1), SUBLANES)):
                    cj = jnp.where(row[i][j] >= thr, a[i], cj)
                if j == 0:
                    for i in range(SUBLANES, PEER_TOPK):
                        cj = jnp.where(tail[i - SUBLANES] >= thr, a[i], cj)
                cut.append(cj)
            for kp in range(ntile // 2):
                r2, pp2 = [], []
                for k in (2 * kp, 2 * kp + 1):
                    rows = slice(k * SUBLANES, (k + 1) * SUBLANES)
                    s1k = tile(s1_ref, k)
                    n1 = jnp.zeros_like(s1k)
                    for j in range(PEER_TOPK):
                        n1 = jnp.where(s1k >= cut[j], float(j + 1), n1)
                    n1_ref[hd, rows, ls] = n1
                    p1_ref[hd, rows, ls] = jnp.exp(s1k - a[0]) * inv_z
                    s2k = tile(s2_ref, k)
                    rk = jnp.full_like(s2k, float(PEER_TOPK))
                    for r in range(PEER_TOPK - 1, -1, -1):
                        rk = jnp.where(s2k >= b[r], float(r), rk)
                    r2.append(rk)
                    pp2.append(jnp.exp(s2k - b[0]))
                rows2 = slice(kp * 2 * SUBLANES, (kp + 1) * 2 * SUBLANES)
                rank2_ref[hd, rows2, ls] = jnp.concatenate(r2, axis=0).astype(BF16)
                p2_ref[hd, rows2, ls] = jnp.concatenate(pp2, axis=0).astype(BF16)
        return carry

    lax.fori_loop(0, PEER_HEADS, head_body, 0)


def _route(h, wq, k1, k2, tm):
    t, d = h.shape
    hk = (PEER_HEADS, PEER_NKEYS, t)
    blk = pl.BlockSpec((PEER_HEADS, PEER_NKEYS, tm), lambda i: (0, 0, i))
    return pl.pallas_call(
        _route_kernel,
        grid=(t // tm,),
        in_specs=[
            pl.BlockSpec((tm, d), lambda i: (i, 0)),
            pl.BlockSpec((PEER_HEADS, d, PEER_DQ), lambda i: (0, 0, 0)),
            pl.BlockSpec((PEER_NKEYS, PEER_DQ_HALF), lambda i: (0, 0)),
            pl.BlockSpec((PEER_NKEYS, PEER_DQ_HALF), lambda i: (0, 0)),
        ],
        out_specs=[pl.BlockSpec((d, tm), lambda i: (0, i)), blk, blk, blk, blk],
        out_shape=[
            jax.ShapeDtypeStruct((d, t), BF16),
            jax.ShapeDtypeStruct(hk, BF16),
            jax.ShapeDtypeStruct(hk, BF16),
            jax.ShapeDtypeStruct(hk, F32),
            jax.ShapeDtypeStruct(hk, F32),
        ],
        scratch_shapes=[pltpu.VMEM((tm, d), BF16),
                        pltpu.VMEM((PEER_NKEYS, tm), F32),
                        pltpu.VMEM((PEER_NKEYS, tm), F32)],
        compiler_params=_params(("parallel",)),
        name="peer_route",
    )(h, wq, k1, k2)


def _peer_kernel(xt_ref, u_ref, vt_ref, rank2_ref, p2_ref, n1_ref, p1_ref, h_ref, g_ref, b_ref,
                 out_ref, acc_ref, y_ref, *, rows):
    e = pl.program_id(1)

    @pl.when(e == 0)
    def _():
        acc_ref[...] = jnp.zeros_like(acc_ref)

    xt = xt_ref[...]
    tm = xt.shape[1]
    pack = 2 * SUBLANES
    slabs = PEER_NKEYS // pack
    zero = jnp.zeros((), BF16)

    def row_tile(ref, hd, l):
        r8 = jnp.broadcast_to(ref[hd, l:l + 1, :], (SUBLANES, tm))
        return jnp.concatenate([r8, r8], axis=0).astype(BF16)[None]

    for l in range(rows):
        es = slice(l * PEER_NKEYS, (l + 1) * PEER_NKEYS)
        act = _gelu(_dot(u_ref[es, :], xt).astype(BF16))
        w = None
        for hd in range(PEER_HEADS):
            keep = rank2_ref[hd].reshape(slabs, pack, tm) < row_tile(n1_ref, hd, l)
            term = jnp.where(keep, p2_ref[hd].reshape(slabs, pack, tm), zero) * row_tile(p1_ref, hd, l)
            w = term if w is None else w + term
        y_ref[es, :] = act * w.reshape(PEER_NKEYS, tm)
    acc_ref[...] += _dot(vt_ref[...], y_ref[...])

    @pl.when(e == pl.num_programs(1) - 1)
    def _():
        ffn = jnp.transpose(acc_ref[...])
        out_ref[...] = _layer_norm(ALPHA * h_ref[...] + ffn, g_ref[...], b_ref[...])


def _peer(xt, u, vt, rank2, p2, n1, p1, h, g, b, tm, te):
    t, d = h.shape
    rows = te // PEER_NKEYS
    kern = functools.partial(_peer_kernel, rows=rows)
    route_blk = pl.BlockSpec((PEER_HEADS, PEER_NKEYS, tm), lambda i, e: (0, 0, i))
    row_blk = pl.BlockSpec((PEER_HEADS, rows, tm), lambda i, e: (0, e, i))
    return pl.pallas_call(
        kern,
        grid=(t // tm, PEER_EXPERTS // te),
        in_specs=[
            pl.BlockSpec((d, tm), lambda i, e: (0, i)),
            pl.BlockSpec((te, d), lambda i, e: (e, 0)),
            pl.BlockSpec((d, te), lambda i, e: (0, e)),
            route_blk, route_blk, row_blk, row_blk,
            pl.BlockSpec((tm, d), lambda i, e: (i, 0)),
            pl.BlockSpec((1, d), lambda i, e: (0, 0)),
            pl.BlockSpec((1, d), lambda i, e: (0, 0)),
        ],
        out_specs=pl.BlockSpec((tm, d), lambda i, e: (i, 0)),
        out_shape=jax.ShapeDtypeStruct((t, d), F32),
        scratch_shapes=[pltpu.VMEM((d, tm), F32), pltpu.VMEM((te, tm), BF16)],
        compiler_params=_params(("parallel", "arbitrary")),
        name="peer_dense",
    )(xt, u, vt, rank2, p2, n1, p1, h, g, b)


def _pack_w_in(w):
    q, k, v, r, a, su, sv = jnp.split(
        w, [256, 512, 1024, 1536, 1552, 2064], axis=-1)
    a = jnp.pad(a, ((0, 0), (0, A_PAD - GLA_GATE_RANK)))
    return jnp.concatenate([q, k, v, r, su, sv, a], axis=-1).astype(BF16)


def _forward(x, ln_in_g, ln_in_b, w_in, w_gate_up, b_gate, gla_norm_g, sgu_ln_g, sgu_ln_b,
             sgu_w, sgu_b, w_out, ln1_g, ln1_b, peer_wq, peer_k1, peer_k2, peer_u, peer_v,
             ln2_g, ln2_b, *, tm_proj, ts_gla, nb_gla, tm_mix, tm_route, tm_peer, te_peer):
    batch, seq, d = x.shape
    t = batch * seq
    row = lambda p: p.reshape(1, -1)
    h = None
    for l in range(DEPTH):
        if l == 0:
            h, p = _in_proj_ln(x.reshape(t, d), ln_in_g, ln_in_b, _pack_w_in(w_in[l]), tm_proj)
        else:
            p = _in_proj(h, _pack_w_in(w_in[l]), tm_proj)
        wg = jnp.pad(w_gate_up[l], ((0, A_PAD - GLA_GATE_RANK), (0, 0)))
        o = _gla(p.reshape(batch, seq, P_COLS), wg, row(b_gate[l]), row(gla_norm_g[l]),
                 batch, seq, ts_gla, nb_gla)
        bs_full = jnp.broadcast_to(sgu_b[l][:, :, None],
                                   (SGU_GROUPS, SGU_WINDOW, SGU_GROUP_DIM))
        h = _mix_out(p, o.reshape(t, GLA_WIDTH), h, row(sgu_ln_g[l]), row(sgu_ln_b[l]),
                     sgu_w[l], bs_full, w_out[l].astype(BF16), row(ln1_g[l]), row(ln1_b[l]),
                     tm_mix)
        wq = peer_wq[l].reshape(d, PEER_HEADS, PEER_DQ).transpose(1, 0, 2).astype(BF16)
        xt, rank2, p2, n1, p1 = _route(h, wq, peer_k1[l].astype(BF16),
                                       peer_k2[l].astype(BF16), tm_route)
        h = _peer(xt, peer_u[l].astype(BF16), jnp.transpose(peer_v[l].astype(BF16)),
                  rank2, p2, n1, p1, h, row(ln2_g[l]), row(ln2_b[l]), tm_peer, te_peer)
    return h.reshape(batch, seq, d)


def kernel(x, ln_in_g, ln_in_b, w_in, w_gate_up, b_gate, gla_norm_g, sgu_ln_g, sgu_ln_b, sgu_w,
           sgu_b, w_out, ln1_g, ln1_b, peer_wq, peer_k1, peer_k2, peer_u, peer_v, ln2_g, ln2_b):
    return _forward(x, ln_in_g, ln_in_b, w_in, w_gate_up, b_gate, gla_norm_g, sgu_ln_g,
                    sgu_ln_b, sgu_w, sgu_b, w_out, ln1_g, ln1_b, peer_wq, peer_k1, peer_k2,
                    peer_u, peer_v, ln2_g, ln2_b,
                    tm_proj=512, ts_gla=256, nb_gla=4, tm_mix=512, tm_route=512, tm_peer=1024,
                    te_peer=1024)
```

```python
import functools
import math

import jax
import jax.numpy as jnp
from jax import lax
from jax.experimental import pallas as pl
from jax.experimental.pallas import tpu as pltpu

F32 = jnp.float32
BF16 = jnp.bfloat16

D_MODEL = 1024
DEPTH = 2
CHUNK = 64
GLA_HEADS = 4
GLA_DK = 64
GLA_DV = 128
GLA_WIDTH = GLA_HEADS * GLA_DV
GLA_QK = GLA_HEADS * GLA_DK
GLA_GATE_RANK = 16
GLA_GATE_NORMALIZER = 16.0
SGU_WIDTH = 512
SGU_GROUPS = 4
SGU_GROUP_DIM = 128
SGU_WINDOW = 128
PEER_HEADS = 8
PEER_NKEYS = 128
PEER_EXPERTS = PEER_NKEYS * PEER_NKEYS
PEER_TOPK = 16
PEER_DQ = 256
PEER_DQ_HALF = 128
LN_EPS = 1e-5
ALPHA = (2.0 * DEPTH) ** 0.25

LANES = 128
SUBLANES = 8

COL_Q = 0
COL_K = COL_Q + GLA_QK
COL_V = COL_K + GLA_QK
COL_R = COL_V + GLA_WIDTH
COL_SU = COL_R + GLA_WIDTH
COL_SV = COL_SU + SGU_WIDTH
COL_A = COL_SV + SGU_WIDTH
A_PAD = LANES
P_COLS = COL_A + A_PAD

SUB = 16
EXP_CLAMP = 80.0
NEG_INF = float("-inf")

VMEM_LIMIT = 56 * 1024 * 1024


def _params(sem, flags=None):
    return pltpu.CompilerParams(dimension_semantics=sem, vmem_limit_bytes=VMEM_LIMIT, flags=flags)


def _layer_norm(x, g, b):
    mu = jnp.mean(x, axis=-1, keepdims=True)
    xc = x - mu
    var = jnp.mean(xc * xc, axis=-1, keepdims=True)
    return xc * lax.rsqrt(var + LN_EPS) * g + b


def _gelu(x):
    return 0.5 * x * (1.0 + lax.erf(x * (1.0 / math.sqrt(2.0))))


def _dot(a, b):
    return jnp.dot(a, b, preferred_element_type=F32)


def _dot_nt(a, b):
    return lax.dot_general(a, b, (((1,), (1,)), ((), ())), preferred_element_type=F32)


def _dot_tn(a, b):
    return lax.dot_general(a, b, (((0,), (0,)), ((), ())), preferred_element_type=F32)


def _in_proj_ln_kernel(x_ref, g_ref, b_ref, w_ref, h_ref, o_ref):
    h = _layer_norm(x_ref[...], g_ref[...], b_ref[...])
    h_ref[...] = h
    o_ref[...] = _dot(h.astype(BF16), w_ref[...])


def _in_proj_ln(x, g, b, w, tm):
    t, d = x.shape
    n = w.shape[1]
    return pl.pallas_call(
        _in_proj_ln_kernel,
        grid=(t // tm,),
        in_specs=[pl.BlockSpec((tm, d), lambda i: (i, 0)),
                  pl.BlockSpec((1, d), lambda i: (0, 0)),
                  pl.BlockSpec((1, d), lambda i: (0, 0)),
                  pl.BlockSpec((d, n), lambda i: (0, 0))],
        out_specs=[pl.BlockSpec((tm, d), lambda i: (i, 0)),
                   pl.BlockSpec((tm, n), lambda i: (i, 0))],
        out_shape=[jax.ShapeDtypeStruct((t, d), F32), jax.ShapeDtypeStruct((t, n), F32)],
        compiler_params=_params(("parallel",)),
        name="in_proj_ln",
    )(x, g.reshape(1, d), b.reshape(1, d), w)


def _in_proj_kernel(h_ref, w_ref, o_ref):
    o_ref[...] = _dot(h_ref[...].astype(BF16), w_ref[...])


def _in_proj(h, w, tm):
    t, d = h.shape
    n = w.shape[1]
    return pl.pallas_call(
        _in_proj_kernel,
        grid=(t // tm,),
        in_specs=[pl.BlockSpec((tm, d), lambda i: (i, 0)),
                  pl.BlockSpec((d, n), lambda i: (0, 0))],
        out_specs=pl.BlockSpec((tm, n), lambda i: (i, 0)),
        out_shape=jax.ShapeDtypeStruct((t, n), F32),
        compiler_params=_params(("parallel",)),
        name="in_proj",
    )(h, w)


def _gla_kernel(q_ref, k_ref, v_ref, r_ref, a_ref, wg_ref, bg_ref, ng_ref, o_ref, st_ref,
                *, chunks, nb):
    @pl.when(pl.program_id(1) == 0)
    def _():
        st_ref[...] = jnp.zeros_like(st_ref)

    ts = chunks * CHUNK
    row = lax.broadcasted_iota(jnp.int32, (CHUNK, CHUNK), 0)
    col = lax.broadcasted_iota(jnp.int32, (CHUNK, CHUNK), 1)
    causal = col <= row
    trow = lax.broadcasted_iota(jnp.int32, (ts, ts), 0)
    tcol = lax.broadcasted_iota(jnp.int32, (ts, ts), 1)
    tril = jnp.logical_and(tcol <= trow, tcol // CHUNK == trow // CHUNK).astype(F32)
    lane_head = lax.broadcasted_iota(jnp.int32, (1, LANES), 1) // GLA_DK
    ng = ng_ref[...]
    groups = GLA_QK // LANES
    per_group = LANES // GLA_DK
    items = [(bi, c) for bi in range(nb) for c in range(chunks)]

    bc = []
    for bi in range(nb):
        z = jnp.dot(a_ref[bi], wg_ref[...], preferred_element_type=F32,
                    precision=lax.Precision.HIGHEST) + bg_ref[...]
        logg = (jnp.minimum(z, 0.0) - jnp.log1p(jnp.exp(-jnp.abs(z)))) * (1.0 / GLA_GATE_NORMALIZER)
        bc.append(jnp.dot(tril, logg, preferred_element_type=F32, precision=lax.Precision.HIGHEST))

    scores = {}
    for bi, c in items:
        rows = slice(c * CHUNK, (c + 1) * CHUNK)
        for grp in range(groups):
            ls = slice(grp * LANES, (grp + 1) * LANES)
            qg = q_ref[bi, rows, ls] * (GLA_DK ** -0.5)
            kg = k_ref[bi, rows, ls]
            bg = bc[bi][rows, ls]
            parts = []
            for i in range(CHUNK // SUB):
                lo, hi = i * SUB, (i + 1) * SUB
                ref = bg[lo:lo + 1, :]
                qi = qg[lo:hi, :] * jnp.exp(bg[lo:hi, :] - ref)
                ki = (kg * jnp.exp(jnp.minimum(ref - bg, EXP_CLAMP))).astype(BF16)
                q2 = jnp.concatenate([jnp.where(lane_head == 0, qi, 0.0),
                                      jnp.where(lane_head == 1, qi, 0.0)], axis=0).astype(BF16)
                parts.append(_dot_nt(q2, ki))
            scores[bi, c, grp] = parts

    intra = {}
    for bi, c in items:
        rows = slice(c * CHUNK, (c + 1) * CHUNK)
        for h in range(GLA_HEADS):
            grp, sub = divmod(h, per_group)
            att = jnp.concatenate([p[sub * SUB:(sub + 1) * SUB, :] for p in scores[bi, c, grp]],
                                  axis=0)
            a_h = jnp.where(causal, att, 0.0).astype(BF16)
            vh_b = v_ref[bi, rows, h * GLA_DV:(h + 1) * GLA_DV].astype(BF16)
            intra[bi, c, h] = _dot(a_h, vh_b)

    for c in range(chunks):
        rows = slice(c * CHUNK, (c + 1) * CHUNK)
        for bi in range(nb):
            for grp in range(groups):
                ls = slice(grp * LANES, (grp + 1) * LANES)
                qg = q_ref[bi, rows, ls] * (GLA_DK ** -0.5)
                kg = k_ref[bi, rows, ls]
                bg = bc[bi][rows, ls]
                b_last = bg[CHUNK - 1:CHUNK, :]
                q_inter = (qg * jnp.exp(bg)).astype(BF16)
                kd = kg * jnp.exp(b_last - bg)
                decay = jnp.exp(b_last)
                for sub in range(per_group):
                    h = grp * per_group + sub
                    vs = slice(h * GLA_DV, (h + 1) * GLA_DV)
                    vh_b = v_ref[bi, rows, vs].astype(BF16)
                    state_t = st_ref[bi, h]
                    o = _dot_nt(q_inter, state_t.astype(BF16)) + intra[bi, c, h]
                    kd_h = jnp.where(lane_head == sub, kd, 0.0).astype(BF16)
                    st_ref[bi, h] = state_t * decay + _dot_tn(vh_b, kd_h)
                    ms = jnp.mean(o * o, axis=-1, keepdims=True)
                    on = o * lax.rsqrt(ms + LN_EPS) * ng
                    rh = r_ref[bi, rows, vs]
                    o_ref[bi, rows, vs] = on * (rh * (1.0 / (1.0 + jnp.exp(-rh))))


def _gla(p, wg, bg, ng, batch, seq, ts, nb):
    kern = functools.partial(_gla_kernel, chunks=ts // CHUNK, nb=nb)
    return pl.pallas_call(
        kern,
        grid=(batch // nb, seq // ts),
        in_specs=[
            pl.BlockSpec((nb, ts, GLA_QK), lambda b, s: (b, s, COL_Q // GLA_QK)),
            pl.BlockSpec((nb, ts, GLA_QK), lambda b, s: (b, s, COL_K // GLA_QK)),
            pl.BlockSpec((nb, ts, GLA_WIDTH), lambda b, s: (b, s, COL_V // GLA_WIDTH)),
            pl.BlockSpec((nb, ts, GLA_WIDTH), lambda b, s: (b, s, COL_R // GLA_WIDTH)),
            pl.BlockSpec((nb, ts, A_PAD), lambda b, s: (b, s, COL_A // A_PAD)),
            pl.BlockSpec((A_PAD, GLA_QK), lambda b, s: (0, 0)),
            pl.BlockSpec((1, GLA_QK), lambda b, s: (0, 0)),
            pl.BlockSpec((1, GLA_DV), lambda b, s: (0, 0)),
        ],
        out_specs=pl.BlockSpec((nb, ts, GLA_WIDTH), lambda b, s: (b, s, 0)),
        out_shape=jax.ShapeDtypeStruct((batch, seq, GLA_WIDTH), F32),
        scratch_shapes=[pltpu.VMEM((nb, GLA_HEADS, GLA_DV, LANES), F32)],
        compiler_params=_params(("parallel", "arbitrary")),
        name="gla",
    )(p, p, p, p, p, wg, bg, ng)


def _mix_out_kernel(su_ref, sv_ref, o_ref, h_ref, lg_ref, lb_ref, ws_ref, bs_ref, wo_ref,
                    g1_ref, b1_ref, out_ref, gate_ref, *, windows):
    u = _gelu(su_ref[...])
    v = _layer_norm(_gelu(sv_ref[...]), lg_ref[...], lb_ref[...])
    blk_r = lax.broadcasted_iota(jnp.int32, (SGU_WINDOW, SGU_WINDOW), 0) // CHUNK
    blk_c = lax.broadcasted_iota(jnp.int32, (SGU_WINDOW, SGU_WINDOW), 1) // CHUNK
    keep = blk_c <= blk_r
    for g in range(SGU_GROUPS):
        cs = slice(g * SGU_GROUP_DIM, (g + 1) * SGU_GROUP_DIM)
        w = jnp.where(keep, ws_ref[g], 0.0).astype(BF16)
        bias = bs_ref[g]
        for n in range(windows):
            rs = slice(n * SGU_WINDOW, (n + 1) * SGU_WINDOW)
            sv = _dot(w, v[rs, cs].astype(BF16)) + bias
            gate_ref[rs, cs] = u[rs, cs] * sv
    mix = _dot(o_ref[...].astype(BF16), wo_ref[0:GLA_WIDTH, :])
    mix = mix + _dot(gate_ref[...].astype(BF16), wo_ref[GLA_WIDTH:, :])
    out_ref[...] = _layer_norm(ALPHA * h_ref[...] + mix, g1_ref[...], b1_ref[...])


def _mix_out(p, o, h, lg, lb, ws, bs_full, wo, g1, b1, tm):
    t, d = h.shape
    kern = functools.partial(_mix_out_kernel, windows=tm // SGU_WINDOW)
    const2 = lambda i: (0, 0)
    const3 = lambda i: (0, 0, 0)
    return pl.pallas_call(
        kern,
        grid=(t // tm,),
        in_specs=[
            pl.BlockSpec((tm, SGU_WIDTH), lambda i: (i, COL_SU // SGU_WIDTH)),
            pl.BlockSpec((tm, SGU_WIDTH), lambda i: (i, COL_SV // SGU_WIDTH)),
            pl.BlockSpec((tm, GLA_WIDTH), lambda i: (i, 0)),
            pl.BlockSpec((tm, d), lambda i: (i, 0)),
            pl.BlockSpec((1, SGU_WIDTH), const2),
            pl.BlockSpec((1, SGU_WIDTH), const2),
            pl.BlockSpec((SGU_GROUPS, SGU_WINDOW, SGU_WINDOW), const3),
            pl.BlockSpec((SGU_GROUPS, SGU_WINDOW, SGU_GROUP_DIM), const3),
            pl.BlockSpec((d, d), const2),
            pl.BlockSpec((1, d), const2),
            pl.BlockSpec((1, d), const2),
        ],
        out_specs=pl.BlockSpec((tm, d), lambda i: (i, 0)),
        out_shape=jax.ShapeDtypeStruct((t, d), F32),
        scratch_shapes=[pltpu.VMEM((tm, SGU_WIDTH), F32)],
        compiler_params=_params(("parallel",)),
        name="mix_out",
    )(p, p, o, h, lg, lb, ws, bs_full, wo, g1, b1)


def _sort_network(n):
    pairs = []

    def merge(lo, hi, r):
        step = r * 2
        if step < hi - lo:
            merge(lo, hi, step)
            merge(lo + r, hi, step)
            pairs.extend((i, i + r) for i in range(lo + r, hi - r, step))
        else:
            pairs.append((lo, lo + r))

    def sort(lo, hi):
        if hi - lo >= 1:
            mid = lo + (hi - lo) // 2
            sort(lo, mid)
            sort(mid + 1, hi)
            merge(lo, hi, 1)

    sort(0, n - 1)
    return pairs


def _ce(x, y):
    if y is None:
        return x, None
    if x is None:
        return y, None
    return jnp.maximum(x, y), jnp.minimum(x, y)


def _merge_top(a, b):
    n = len(a)
    v = [_ce(a[k], b[n - 1 - k])[0] for k in range(n)]
    d = n // 2
    while d >= 1:
        for i in range(n):
            if i & d == 0:
                v[i], v[i + d] = _ce(v[i], v[i + d])
        d //= 2
    return v


def _top_sorted(tiles):
    v = list(tiles)
    for i, j in _sort_network(len(v)):
        v[i], v[j] = _ce(v[i], v[j])
    shift = SUBLANES // 2
    while shift >= 1:
        v = _merge_top(v, [pltpu.roll(x, shift, axis=0) for x in v])
        shift //= 2
    return v


def _pad(vals):
    return list(vals) + [None] * (PEER_TOPK - len(vals))


def _fold_kernel(k_ref, wq_ref, o_ref):
    o_ref[...] = lax.dot_general(k_ref[...], wq_ref[...], (((1,), (1,)), ((), ())),
                                 preferred_element_type=F32,
                                 precision=lax.Precision.HIGHEST).astype(BF16)


def _fold_keys(k1, k2, wq):
    d = wq.shape[0]
    keys = jnp.stack([k1, k2])
    return pl.pallas_call(
        _fold_kernel,
        grid=(PEER_HEADS, 2),
        in_specs=[pl.BlockSpec((None, PEER_NKEYS, PEER_DQ_HALF), lambda h, s: (s, 0, 0)),
                  pl.BlockSpec((d, PEER_DQ_HALF), lambda h, s: (0, 2 * h + s))],
        out_specs=pl.BlockSpec((PEER_NKEYS, d), lambda h, s: (2 * h + s, 0)),
        out_shape=jax.ShapeDtypeStruct((PEER_HEADS * 2 * PEER_NKEYS, d), BF16),
        compiler_params=_params(("parallel", "parallel")),
        name="peer_fold",
    )(keys, wq)


def _route_kernel(h_ref, wall_ref, xt_ref, rank2_ref, p2_ref, n1_ref, p1_ref,
                  s_ref, b_ref, a0_ref, pa_ref, pb_ref, cut_ref, invz_ref):
    tm = h_ref.shape[0]
    xt = jnp.transpose(h_ref[...]).astype(BF16)
    xt_ref[...] = xt
    s_ref[...] = _dot(wall_ref[...], xt)
    ntile = PEER_NKEYS // SUBLANES

    for c in range(tm // LANES):
        ls = slice(c * LANES, (c + 1) * LANES)

        def tile(hd, half, k):
            r0 = pl.multiple_of(hd * (2 * PEER_NKEYS) + half * PEER_NKEYS + k * SUBLANES, SUBLANES)
            return s_ref[pl.ds(r0, SUBLANES), ls]

        def sort_body(hd, carry):
            a = _top_sorted([tile(hd, 0, k) for k in range(ntile)])
            b = _top_sorted([tile(hd, 1, k) for k in range(ntile)])
            a0_ref[hd] = a[0]
            for r in range(PEER_TOPK):
                b_ref[hd, r] = b[r]
                pa_ref[r, pl.ds(hd, 1), :] = a[r][0:1, :]
                pb_ref[r, pl.ds(hd, 1), :] = b[r][0:1, :]
            return carry

        lax.fori_loop(0, PEER_HEADS, sort_body, 0)

        a = [pa_ref[r] for r in range(PEER_TOPK)]
        b = [pb_ref[r] for r in range(PEER_TOPK)]
        row = [_pad([a[i] + b[j] for j in range(PEER_TOPK // (i + 1))]) for i in range(SUBLANES)]
        tail = _pad([a[i] + b[0] for i in range(SUBLANES, PEER_TOPK)])
        small = _merge_top(_merge_top(row[4], row[5]), _merge_top(row[6], row[7]))
        mid = _merge_top(_merge_top(row[2], row[3]), small)
        top = _merge_top(row[0], _merge_top(_merge_top(row[1], tail), mid))
        thr = top[PEER_TOPK - 1]
        z = None
        for r in range(PEER_TOPK):
            e = jnp.exp(top[r] - top[0])
            z = e if z is None else z + e
        invz_ref[...] = 1.0 / z
        for j in range(PEER_TOPK):
            cj = jnp.full_like(thr, float("inf"))
            for i in range(min(PEER_TOPK // (j + 1), SUBLANES)):
                cj = jnp.where(row[i][j] >= thr, a[i], cj)
            if j == 0:
                for i in range(SUBLANES, PEER_TOPK):
                    cj = jnp.where(tail[i - SUBLANES] >= thr, a[i], cj)
            cut_ref[j] = cj

        def emit_body(hd, carry):
            head_row = lambda ref, *idx: jnp.broadcast_to(ref[(*idx, pl.ds(hd, 1), slice(None))],
                                                          (SUBLANES, LANES))
            cut = [head_row(cut_ref, j) for j in range(PEER_TOPK)]
            inv_z = head_row(invz_ref)
            a0 = a0_ref[hd]
            bs = [b_ref[hd, r] for r in range(PEER_TOPK)]
            for kp in range(ntile // 2):
                r2, pp2 = [], []
                for k in (2 * kp, 2 * kp + 1):
                    rows = slice(k * SUBLANES, (k + 1) * SUBLANES)
                    s1k = tile(hd, 0, k)
                    n1 = jnp.zeros_like(s1k)
                    for j in range(PEER_TOPK):
                        n1 = jnp.where(s1k >= cut[j], float(j + 1), n1)
                    n1_ref[hd, rows, ls] = n1
                    p1_ref[hd, rows, ls] = jnp.exp(s1k - a0) * inv_z
                    s2k = tile(hd, 1, k)
                    rk = jnp.full_like(s2k, float(PEER_TOPK))
                    for r in range(PEER_TOPK - 1, -1, -1):
                        rk = jnp.where(s2k >= bs[r], float(r), rk)
                    r2.append(rk)
                    pp2.append(jnp.exp(s2k - bs[0]))
                rows2 = slice(kp * 2 * SUBLANES, (kp + 1) * 2 * SUBLANES)
                rank2_ref[hd, rows2, ls] = jnp.concatenate(r2, axis=0).astype(BF16)
                p2_ref[hd, rows2, ls] = jnp.concatenate(pp2, axis=0).astype(BF16)
            return carry

        lax.fori_loop(0, PEER_HEADS, emit_body, 0)


def _route(h, wall, tm):
    t, d = h.shape
    hk = (PEER_HEADS, PEER_NKEYS, t)
    blk = pl.BlockSpec((PEER_HEADS, PEER_NKEYS, tm), lambda i: (0, 0, i))
    vreg = (SUBLANES, LANES)
    return pl.pallas_call(
        _route_kernel,
        grid=(t // tm,),
        in_specs=[
            pl.BlockSpec((tm, d), lambda i: (i, 0)),
            pl.BlockSpec(wall.shape, lambda i: (0, 0)),
        ],
        out_specs=[pl.BlockSpec((d, tm), lambda i: (0, i)), blk, blk, blk, blk],
        out_shape=[
            jax.ShapeDtypeStruct((d, t), BF16),
            jax.ShapeDtypeStruct(hk, BF16),
            jax.ShapeDtypeStruct(hk, BF16),
            jax.ShapeDtypeStruct(hk, F32),
            jax.ShapeDtypeStruct(hk, F32),
        ],
        scratch_shapes=[pltpu.VMEM((wall.shape[0], tm), F32),
                        pltpu.VMEM((PEER_HEADS, PEER_TOPK) + vreg, F32),
                        pltpu.VMEM((PEER_HEADS,) + vreg, F32),
                        pltpu.VMEM((PEER_TOPK,) + vreg, F32),
                        pltpu.VMEM((PEER_TOPK,) + vreg, F32),
                        pltpu.VMEM((PEER_TOPK,) + vreg, F32),
                        pltpu.VMEM(vreg, F32)],
        compiler_params=_params(("parallel",)),
        name="peer_route",
    )(h, wall)


def _peer_kernel(xt_ref, u_ref, vt_ref, rank2_ref, p2_ref, n1_ref, p1_ref, h_ref, g_ref, b_ref,
                 out_ref, acc_ref, y_ref, *, rows):
    e = pl.program_id(1)

    @pl.when(e == 0)
    def _():
        acc_ref[...] = jnp.zeros_like(acc_ref)

    xt = xt_ref[...]
    tm = xt.shape[1]
    pack = 2 * SUBLANES
    slabs = PEER_NKEYS // pack
    zero = jnp.zeros((), BF16)

    def row_tile(ref, hd, l):
        r8 = jnp.broadcast_to(ref[hd, l:l + 1, :], (SUBLANES, tm))
        return jnp.concatenate([r8, r8], axis=0).astype(BF16)[None]

    for l in range(rows):
        es = slice(l * PEER_NKEYS, (l + 1) * PEER_NKEYS)
        act = _gelu(_dot(u_ref[es, :], xt).astype(BF16))
        w = None
        for hd in range(PEER_HEADS):
            keep = rank2_ref[hd].reshape(slabs, pack, tm) < row_tile(n1_ref, hd, l)
            term = jnp.where(keep, p2_ref[hd].reshape(slabs, pack, tm), zero) * row_tile(p1_ref, hd, l)
            w = term if w is None else w + term
        y_ref[es, :] = act * w.reshape(PEER_NKEYS, tm)
    acc_ref[...] += _dot(vt_ref[...], y_ref[...])

    @pl.when(e == pl.num_programs(1) - 1)
    def _():
        ffn = jnp.transpose(acc_ref[...])
        out_ref[...] = _layer_norm(ALPHA * h_ref[...] + ffn, g_ref[...], b_ref[...])


def _peer(xt, u, vt, rank2, p2, n1, p1, h, g, b, tm, te):
    t, d = h.shape
    rows = te // PEER_NKEYS
    kern = functools.partial(_peer_kernel, rows=rows)
    route_blk = pl.BlockSpec((PEER_HEADS, PEER_NKEYS, tm), lambda i, e: (0, 0, i))
    row_blk = pl.BlockSpec((PEER_HEADS, rows, tm), lambda i, e: (0, e, i))
    return pl.pallas_call(
        kern,
        grid=(t // tm, PEER_EXPERTS // te),
        in_specs=[
            pl.BlockSpec((d, tm), lambda i, e: (0, i)),
            pl.BlockSpec((te, d), lambda i, e: (e, 0)),
            pl.BlockSpec((d, te), lambda i, e: (0, e)),
            route_blk, route_blk, row_blk, row_blk,
            pl.BlockSpec((tm, d), lambda i, e: (i, 0)),
            pl.BlockSpec((1, d), lambda i, e: (0, 0)),
            pl.BlockSpec((1, d), lambda i, e: (0, 0)),
        ],
        out_specs=pl.BlockSpec((tm, d), lambda i, e: (i, 0)),
        out_shape=jax.ShapeDtypeStruct((t, d), F32),
        scratch_shapes=[pltpu.VMEM((d, tm), F32), pltpu.VMEM((te, tm), BF16)],
        compiler_params=_params(("parallel", "arbitrary")),
        name="peer_dense",
    )(xt, u, vt, rank2, p2, n1, p1, h, g, b)


def _pack_w_in(w):
    q, k, v, r, a, su, sv = jnp.split(
        w, [256, 512, 1024, 1536, 1552, 2064], axis=-1)
    a = jnp.pad(a, ((0, 0), (0, A_PAD - GLA_GATE_RANK)))
    return jnp.concatenate([q, k, v, r, su, sv, a], axis=-1).astype(BF16)


def _forward(x, ln_in_g, ln_in_b, w_in, w_gate_up, b_gate, gla_norm_g, sgu_ln_g, sgu_ln_b,
             sgu_w, sgu_b, w_out, ln1_g, ln1_b, peer_wq, peer_k1, peer_k2, peer_u, peer_v,
             ln2_g, ln2_b, *, tm_proj, ts_gla, nb_gla, tm_mix, tm_route, tm_peer, te_peer):
    batch, seq, d = x.shape
    t = batch * seq
    row = lambda p: p.reshape(1, -1)
    h = None
    for l in range(DEPTH):
        if l == 0:
            h, p = _in_proj_ln(x.reshape(t, d), ln_in_g, ln_in_b, _pack_w_in(w_in[l]), tm_proj)
        else:
            p = _in_proj(h, _pack_w_in(w_in[l]), tm_proj)
        wg = jnp.pad(w_gate_up[l], ((0, A_PAD - GLA_GATE_RANK), (0, 0)))
        o = _gla(p.reshape(batch, seq, P_COLS), wg, row(b_gate[l]), row(gla_norm_g[l]),
                 batch, seq, ts_gla, nb_gla)
        bs_full = jnp.broadcast_to(sgu_b[l][:, :, None],
                                   (SGU_GROUPS, SGU_WINDOW, SGU_GROUP_DIM))
        h = _mix_out(p, o.reshape(t, GLA_WIDTH), h, row(sgu_ln_g[l]), row(sgu_ln_b[l]),
                     sgu_w[l], bs_full, w_out[l].astype(BF16), row(ln1_g[l]), row(ln1_b[l]),
                     tm_mix)
        wall = _fold_keys(peer_k1[l], peer_k2[l], peer_wq[l])
        xt, rank2, p2, n1, p1 = _route(h, wall, tm_route)
        h = _peer(xt, peer_u[l].astype(BF16), jnp.transpose(peer_v[l].astype(BF16)),
                  rank2, p2, n1, p1, h, row(ln2_g[l]), row(ln2_b[l]), tm_peer, te_peer)
    return h.reshape(batch, seq, d)


def kernel(x, ln_in_g, ln_in_b, w_in, w_gate_up, b_gate, gla_norm_g, sgu_ln_g, sgu_ln_b, sgu_w,
           sgu_b, w_out, ln1_g, ln1_b, peer_wq, peer_k1, peer_k2, peer_u, peer_v, ln2_g, ln2_b):
    return _forward(x, ln_in_g, ln_in_b, w_in, w_gate_up, b_gate, gla_norm_g, sgu_ln_g,
                    sgu_ln_b, sgu_w, sgu_b, w_out, ln1_g, ln1_b, peer_wq, peer_k1, peer_k2,
                    peer_u, peer_v, ln2_g, ln2_b,
                    tm_proj=512, ts_gla=256, nb_gla=4, tm_mix=512, tm_route=512, tm_peer=1024,
                    te_peer=1024)
```

```python
import functools
import math

import jax
import jax.numpy as jnp
from jax import lax
from jax.experimental import pallas as pl
from jax.experimental.pallas import tpu as pltpu

F32 = jnp.float32
BF16 = jnp.bfloat16

D_MODEL = 1024
DEPTH = 2
CHUNK = 64
GLA_HEADS = 4
GLA_DK = 64
GLA_DV = 128
GLA_WIDTH = GLA_HEADS * GLA_DV
GLA_QK = GLA_HEADS * GLA_DK
GLA_GATE_RANK = 16
GLA_GATE_NORMALIZER = 16.0
SGU_WIDTH = 512
SGU_GROUPS = 4
SGU_GROUP_DIM = 128
SGU_WINDOW = 128
PEER_HEADS = 8
PEER_NKEYS = 128
PEER_EXPERTS = PEER_NKEYS * PEER_NKEYS
PEER_TOPK = 16
PEER_DQ = 256
PEER_DQ_HALF = 128
LN_EPS = 1e-5
ALPHA = (2.0 * DEPTH) ** 0.25

LANES = 128
SUBLANES = 8

COL_Q = 0
COL_K = COL_Q + GLA_QK
COL_V = COL_K + GLA_QK
COL_R = COL_V + GLA_WIDTH
COL_SU = COL_R + GLA_WIDTH
COL_SV = COL_SU + SGU_WIDTH
COL_A = COL_SV + SGU_WIDTH
A_PAD = LANES
P_COLS = COL_A + A_PAD

SUB = 16
EXP_CLAMP = 80.0

V7X_VMEM_BYTES = 64 * 1024 * 1024
VMEM_LIMIT = V7X_VMEM_BYTES * 7 // 8


def _params(sem):
    return pltpu.CompilerParams(dimension_semantics=sem, vmem_limit_bytes=VMEM_LIMIT)


def _layer_norm(x, g, b):
    mu = jnp.mean(x, axis=-1, keepdims=True)
    xc = x - mu
    var = jnp.mean(xc * xc, axis=-1, keepdims=True)
    return xc * lax.rsqrt(var + LN_EPS) * g + b


def _gelu(x):
    return 0.5 * x * (1.0 + lax.erf(x * (1.0 / math.sqrt(2.0))))


def _dot(a, b):
    return jnp.dot(a, b, preferred_element_type=F32)


def _dot_nt(a, b):
    return lax.dot_general(a, b, (((1,), (1,)), ((), ())), preferred_element_type=F32)


def _dot_tn(a, b):
    return lax.dot_general(a, b, (((0,), (0,)), ((), ())), preferred_element_type=F32)


def _in_proj_ln_kernel(x_ref, g_ref, b_ref, w_ref, h_ref, o_ref):
    h = _layer_norm(x_ref[...], g_ref[...], b_ref[...])
    h_ref[...] = h
    o_ref[...] = _dot(h.astype(BF16), w_ref[...])


def _in_proj_ln(x, g, b, w, tm):
    t, d = x.shape
    n = w.shape[1]
    return pl.pallas_call(
        _in_proj_ln_kernel,
        grid=(t // tm,),
        in_specs=[pl.BlockSpec((tm, d), lambda i: (i, 0)),
                  pl.BlockSpec((1, d), lambda i: (0, 0)),
                  pl.BlockSpec((1, d), lambda i: (0, 0)),
                  pl.BlockSpec((d, n), lambda i: (0, 0))],
        out_specs=[pl.BlockSpec((tm, d), lambda i: (i, 0)),
                   pl.BlockSpec((tm, n), lambda i: (i, 0))],
        out_shape=[jax.ShapeDtypeStruct((t, d), F32), jax.ShapeDtypeStruct((t, n), F32)],
        compiler_params=_params(("parallel",)),
        name="in_proj_ln",
    )(x, g.reshape(1, d), b.reshape(1, d), w)


def _in_proj_kernel(h_ref, w_ref, o_ref):
    o_ref[...] = _dot(h_ref[...].astype(BF16), w_ref[...])


def _in_proj(h, w, tm):
    t, d = h.shape
    n = w.shape[1]
    return pl.pallas_call(
        _in_proj_kernel,
        grid=(t // tm,),
        in_specs=[pl.BlockSpec((tm, d), lambda i: (i, 0)),
                  pl.BlockSpec((d, n), lambda i: (0, 0))],
        out_specs=pl.BlockSpec((tm, n), lambda i: (i, 0)),
        out_shape=jax.ShapeDtypeStruct((t, n), F32),
        compiler_params=_params(("parallel",)),
        name="in_proj",
    )(h, w)


def _gla_kernel(q_ref, k_ref, v_ref, r_ref, a_ref, wg_ref, bg_ref, ng_ref, o_ref, st_ref,
                *, chunks, nb):
    @pl.when(pl.program_id(1) == 0)
    def _():
        st_ref[...] = jnp.zeros_like(st_ref)

    ts = chunks * CHUNK
    row = lax.broadcasted_iota(jnp.int32, (CHUNK, CHUNK), 0)
    col = lax.broadcasted_iota(jnp.int32, (CHUNK, CHUNK), 1)
    causal = col <= row
    trow = lax.broadcasted_iota(jnp.int32, (ts, ts), 0)
    tcol = lax.broadcasted_iota(jnp.int32, (ts, ts), 1)
    tril = jnp.logical_and(tcol <= trow, tcol // CHUNK == trow // CHUNK).astype(F32)
    lane_head = lax.broadcasted_iota(jnp.int32, (1, LANES), 1) // GLA_DK
    ng = ng_ref[...]
    groups = GLA_QK // LANES
    per_group = LANES // GLA_DK
    items = [(bi, c) for bi in range(nb) for c in range(chunks)]

    bc = []
    for bi in range(nb):
        z = jnp.dot(a_ref[bi], wg_ref[...], preferred_element_type=F32,
                    precision=lax.Precision.HIGHEST) + bg_ref[...]
        logg = (jnp.minimum(z, 0.0) - jnp.log1p(jnp.exp(-jnp.abs(z)))) * (1.0 / GLA_GATE_NORMALIZER)
        bc.append(jnp.dot(tril, logg, preferred_element_type=F32, precision=lax.Precision.HIGHEST))

    scores = {}
    for bi, c in items:
        rows = slice(c * CHUNK, (c + 1) * CHUNK)
        for grp in range(groups):
            ls = slice(grp * LANES, (grp + 1) * LANES)
            qg = q_ref[bi, rows, ls] * (GLA_DK ** -0.5)
            kg = k_ref[bi, rows, ls]
            bg = bc[bi][rows, ls]
            parts = []
            for i in range(CHUNK // SUB):
                lo, hi = i * SUB, (i + 1) * SUB
                ref = bg[lo:lo + 1, :]
                qi = qg[lo:hi, :] * jnp.exp(bg[lo:hi, :] - ref)
                ki = (kg * jnp.exp(jnp.minimum(ref - bg, EXP_CLAMP))).astype(BF16)
                q2 = jnp.concatenate([jnp.where(lane_head == 0, qi, 0.0),
                                      jnp.where(lane_head == 1, qi, 0.0)], axis=0).astype(BF16)
                parts.append(_dot_nt(q2, ki))
            scores[bi, c, grp] = parts

    intra = {}
    for bi, c in items:
        rows = slice(c * CHUNK, (c + 1) * CHUNK)
        for h in range(GLA_HEADS):
            grp, sub = divmod(h, per_group)
            att = jnp.concatenate([p[sub * SUB:(sub + 1) * SUB, :] for p in scores[bi, c, grp]],
                                  axis=0)
            a_h = jnp.where(causal, att, 0.0).astype(BF16)
            vh_b = v_ref[bi, rows, h * GLA_DV:(h + 1) * GLA_DV].astype(BF16)
            intra[bi, c, h] = _dot(a_h, vh_b)

    for c in range(chunks):
        rows = slice(c * CHUNK, (c + 1) * CHUNK)
        for bi in range(nb):
            for grp in range(groups):
                ls = slice(grp * LANES, (grp + 1) * LANES)
                qg = q_ref[bi, rows, ls] * (GLA_DK ** -0.5)
                kg = k_ref[bi, rows, ls]
                bg = bc[bi][rows, ls]
                b_last = bg[CHUNK - 1:CHUNK, :]
                q_inter = (qg * jnp.exp(bg)).astype(BF16)
                kd = kg * jnp.exp(b_last - bg)
                decay = jnp.exp(b_last)
                for sub in range(per_group):
                    h = grp * per_group + sub
                    vs = slice(h * GLA_DV, (h + 1) * GLA_DV)
                    vh_b = v_ref[bi, rows, vs].astype(BF16)
                    state_t = st_ref[bi, h]
                    o = _dot_nt(q_inter, state_t.astype(BF16)) + intra[bi, c, h]
                    kd_h = jnp.where(lane_head == sub, kd, 0.0).astype(BF16)
                    st_ref[bi, h] = state_t * decay + _dot_tn(vh_b, kd_h)
                    ms = jnp.mean(o * o, axis=-1, keepdims=True)
                    on = o * lax.rsqrt(ms + LN_EPS) * ng
                    rh = r_ref[bi, rows, vs]
                    o_ref[bi, rows, vs] = on * (rh * (1.0 / (1.0 + jnp.exp(-rh))))


def _gla(p, wg, bg, ng, batch, seq, ts, nb):
    kern = functools.partial(_gla_kernel, chunks=ts // CHUNK, nb=nb)
    return pl.pallas_call(
        kern,
        grid=(batch // nb, seq // ts),
        in_specs=[
            pl.BlockSpec((nb, ts, GLA_QK), lambda b, s: (b, s, COL_Q // GLA_QK)),
            pl.BlockSpec((nb, ts, GLA_QK), lambda b, s: (b, s, COL_K // GLA_QK)),
            pl.BlockSpec((nb, ts, GLA_WIDTH), lambda b, s: (b, s, COL_V // GLA_WIDTH)),
            pl.BlockSpec((nb, ts, GLA_WIDTH), lambda b, s: (b, s, COL_R // GLA_WIDTH)),
            pl.BlockSpec((nb, ts, A_PAD), lambda b, s: (b, s, COL_A // A_PAD)),
            pl.BlockSpec((A_PAD, GLA_QK), lambda b, s: (0, 0)),
            pl.BlockSpec((1, GLA_QK), lambda b, s: (0, 0)),
            pl.BlockSpec((1, GLA_DV), lambda b, s: (0, 0)),
        ],
        out_specs=pl.BlockSpec((nb, ts, GLA_WIDTH), lambda b, s: (b, s, 0)),
        out_shape=jax.ShapeDtypeStruct((batch, seq, GLA_WIDTH), F32),
        scratch_shapes=[pltpu.VMEM((nb, GLA_HEADS, GLA_DV, LANES), F32)],
        compiler_params=_params(("parallel", "arbitrary")),
        name="gla",
    )(p, p, p, p, p, wg, bg, ng)


def _mix_out_kernel(su_ref, sv_ref, o_ref, h_ref, lg_ref, lb_ref, ws_ref, bs_ref, wo_ref,
                    g1_ref, b1_ref, out_ref, gate_ref, *, windows):
    u = _gelu(su_ref[...])
    v = _layer_norm(_gelu(sv_ref[...]), lg_ref[...], lb_ref[...])
    blk_r = lax.broadcasted_iota(jnp.int32, (SGU_WINDOW, SGU_WINDOW), 0) // CHUNK
    blk_c = lax.broadcasted_iota(jnp.int32, (SGU_WINDOW, SGU_WINDOW), 1) // CHUNK
    keep = blk_c <= blk_r
    for g in range(SGU_GROUPS):
        cs = slice(g * SGU_GROUP_DIM, (g + 1) * SGU_GROUP_DIM)
        w = jnp.where(keep, ws_ref[g], 0.0).astype(BF16)
        bias = bs_ref[g]
        for n in range(windows):
            rs = slice(n * SGU_WINDOW, (n + 1) * SGU_WINDOW)
            sv = _dot(w, v[rs, cs].astype(BF16)) + bias
            gate_ref[rs, cs] = u[rs, cs] * sv
    mix = _dot(o_ref[...].astype(BF16), wo_ref[0:GLA_WIDTH, :])
    mix = mix + _dot(gate_ref[...].astype(BF16), wo_ref[GLA_WIDTH:, :])
    out_ref[...] = _layer_norm(ALPHA * h_ref[...] + mix, g1_ref[...], b1_ref[...])


def _mix_out(p, o, h, lg, lb, ws, bs_full, wo, g1, b1, tm):
    t, d = h.shape
    kern = functools.partial(_mix_out_kernel, windows=tm // SGU_WINDOW)
    const2 = lambda i: (0, 0)
    const3 = lambda i: (0, 0, 0)
    return pl.pallas_call(
        kern,
        grid=(t // tm,),
        in_specs=[
            pl.BlockSpec((tm, SGU_WIDTH), lambda i: (i, COL_SU // SGU_WIDTH)),
            pl.BlockSpec((tm, SGU_WIDTH), lambda i: (i, COL_SV // SGU_WIDTH)),
            pl.BlockSpec((tm, GLA_WIDTH), lambda i: (i, 0)),
            pl.BlockSpec((tm, d), lambda i: (i, 0)),
            pl.BlockSpec((1, SGU_WIDTH), const2),
            pl.BlockSpec((1, SGU_WIDTH), const2),
            pl.BlockSpec((SGU_GROUPS, SGU_WINDOW, SGU_WINDOW), const3),
            pl.BlockSpec((SGU_GROUPS, SGU_WINDOW, SGU_GROUP_DIM), const3),
            pl.BlockSpec((d, d), const2),
            pl.BlockSpec((1, d), const2),
            pl.BlockSpec((1, d), const2),
        ],
        out_specs=pl.BlockSpec((tm, d), lambda i: (i, 0)),
        out_shape=jax.ShapeDtypeStruct((t, d), F32),
        scratch_shapes=[pltpu.VMEM((tm, SGU_WIDTH), F32)],
        compiler_params=_params(("parallel",)),
        name="mix_out",
    )(p, p, o, h, lg, lb, ws, bs_full, wo, g1, b1)


def _sort_network(n):
    pairs = []

    def merge(lo, hi, r):
        step = r * 2
        if step < hi - lo:
            merge(lo, hi, step)
            merge(lo + r, hi, step)
            pairs.extend((i, i + r) for i in range(lo + r, hi - r, step))
        else:
            pairs.append((lo, lo + r))

    def sort(lo, hi):
        if hi - lo >= 1:
            mid = lo + (hi - lo) // 2
            sort(lo, mid)
            sort(mid + 1, hi)
            merge(lo, hi, 1)

    sort(0, n - 1)
    return pairs


def _ce(x, y):
    if y is None:
        return x, None
    if x is None:
        return y, None
    return jnp.maximum(x, y), jnp.minimum(x, y)


def _merge_top(a, b):
    n = len(a)
    v = [_ce(a[k], b[n - 1 - k])[0] for k in range(n)]
    d = n // 2
    while d >= 1:
        for i in range(n):
            if i & d == 0:
                v[i], v[i + d] = _ce(v[i], v[i + d])
        d //= 2
    return v


def _top_sorted(tiles):
    v = list(tiles)
    for i, j in _sort_network(len(v)):
        v[i], v[j] = _ce(v[i], v[j])
    shift = SUBLANES // 2
    while shift >= 1:
        v = _merge_top(v, [pltpu.roll(x, shift, axis=0) for x in v])
        shift //= 2
    return v


def _pad(vals):
    return list(vals) + [None] * (PEER_TOPK - len(vals))


def _fold_kernel(k_ref, wq_ref, o_ref):
    o_ref[...] = lax.dot_general(k_ref[...], wq_ref[...], (((1,), (1,)), ((), ())),
                                 preferred_element_type=F32,
                                 precision=lax.Precision.HIGHEST).astype(BF16)


def _fold_keys(k1, k2, wq):
    d = wq.shape[0]
    keys = jnp.stack([k1, k2])
    return pl.pallas_call(
        _fold_kernel,
        grid=(PEER_HEADS, 2),
        in_specs=[pl.BlockSpec((None, PEER_NKEYS, PEER_DQ_HALF), lambda h, s: (s, 0, 0)),
                  pl.BlockSpec((d, PEER_DQ_HALF), lambda h, s: (0, 2 * h + s))],
        out_specs=pl.BlockSpec((PEER_NKEYS, d), lambda h, s: (2 * h + s, 0)),
        out_shape=jax.ShapeDtypeStruct((PEER_HEADS * 2 * PEER_NKEYS, d), BF16),
        compiler_params=_params(("parallel", "parallel")),
        name="peer_fold",
    )(keys, wq)


def _route_kernel(h_ref, wall_ref, xt_ref, rank2_ref, p2_ref, n1_ref, p1_ref,
                  s_ref, b_ref, a0_ref, pa_ref, pb_ref, cut_ref, invz_ref):
    tm = h_ref.shape[0]
    xt = jnp.transpose(h_ref[...]).astype(BF16)
    xt_ref[...] = xt
    s_ref[...] = _dot(wall_ref[...], xt)
    ntile = PEER_NKEYS // SUBLANES

    for c in range(tm // LANES):
        ls = slice(c * LANES, (c + 1) * LANES)

        def tile(hd, half, k):
            r0 = pl.multiple_of(hd * (2 * PEER_NKEYS) + half * PEER_NKEYS + k * SUBLANES, SUBLANES)
            return s_ref[pl.ds(r0, SUBLANES), ls]

        def sort_body(hd, carry):
            a = _top_sorted([tile(hd, 0, k) for k in range(ntile)])
            b = _top_sorted([tile(hd, 1, k) for k in range(ntile)])
            a0_ref[hd] = a[0]
            for r in range(PEER_TOPK):
                b_ref[hd, r] = b[r]
                pa_ref[r, pl.ds(hd, 1), :] = a[r][0:1, :]
                pb_ref[r, pl.ds(hd, 1), :] = b[r][0:1, :]
            return carry

        lax.fori_loop(0, PEER_HEADS, sort_body, 0)

        a = [pa_ref[r] for r in range(PEER_TOPK)]
        b = [pb_ref[r] for r in range(PEER_TOPK)]
        row = [_pad([a[i] + b[j] for j in range(PEER_TOPK // (i + 1))]) for i in range(SUBLANES)]
        tail = _pad([a[i] + b[0] for i in range(SUBLANES, PEER_TOPK)])
        small = _merge_top(_merge_top(row[4], row[5]), _merge_top(row[6], row[7]))
        mid = _merge_top(_merge_top(row[2], row[3]), small)
        top = _merge_top(row[0], _merge_top(_merge_top(row[1], tail), mid))
        thr = top[PEER_TOPK - 1]
        z = None
        for r in range(PEER_TOPK):
            e = jnp.exp(top[r] - top[0])
            z = e if z is None else z + e
        invz_ref[...] = 0.5 / z
        for j in range(PEER_TOPK):
            cj = jnp.full_like(thr, float("inf"))
            for i in range(min(PEER_TOPK // (j + 1), SUBLANES)):
                cj = jnp.where(row[i][j] >= thr, a[i], cj)
            if j == 0:
                for i in range(SUBLANES, PEER_TOPK):
                    cj = jnp.where(tail[i - SUBLANES] >= thr, a[i], cj)
            cut_ref[j] = cj

        def emit_body(hd, carry):
            head_row = lambda ref, *idx: jnp.broadcast_to(ref[(*idx, pl.ds(hd, 1), slice(None))],
                                                          (SUBLANES, LANES))
            cut = [head_row(cut_ref, j) for j in range(PEER_TOPK)]
            inv_z = head_row(invz_ref)
            a0 = a0_ref[hd]
            bs = [b_ref[hd, r] for r in range(PEER_TOPK)]
            for kp in range(ntile // 2):
                r2, pp2 = [], []
                for k in (2 * kp, 2 * kp + 1):
                    rows = slice(k * SUBLANES, (k + 1) * SUBLANES)
                    s1k = tile(hd, 0, k)
                    n1 = jnp.zeros_like(s1k)
                    for j in range(PEER_TOPK):
                        n1 = jnp.where(s1k >= cut[j], float(j + 1), n1)
                    n1_ref[hd, rows, ls] = n1
                    p1_ref[hd, rows, ls] = jnp.exp(s1k - a0) * inv_z
                    s2k = tile(hd, 1, k)
                    rk = jnp.full_like(s2k, float(PEER_TOPK))
                    for r in range(PEER_TOPK - 1, -1, -1):
                        rk = jnp.where(s2k >= bs[r], float(r), rk)
                    r2.append(rk)
                    pp2.append(jnp.exp(s2k - bs[0]))
                rows2 = slice(kp * 2 * SUBLANES, (kp + 1) * 2 * SUBLANES)
                rank2_ref[hd, rows2, ls] = jnp.concatenate(r2, axis=0).astype(BF16)
                p2_ref[hd, rows2, ls] = jnp.concatenate(pp2, axis=0).astype(BF16)
            return carry

        lax.fori_loop(0, PEER_HEADS, emit_body, 0)


def _route(h, wall, tm):
    t, d = h.shape
    hk = (PEER_HEADS, PEER_NKEYS, t)
    blk = pl.BlockSpec((PEER_HEADS, PEER_NKEYS, tm), lambda i: (0, 0, i))
    vreg = (SUBLANES, LANES)
    return pl.pallas_call(
        _route_kernel,
        grid=(t // tm,),
        in_specs=[
            pl.BlockSpec((tm, d), lambda i: (i, 0)),
            pl.BlockSpec(wall.shape, lambda i: (0, 0)),
        ],
        out_specs=[pl.BlockSpec((d, tm), lambda i: (0, i)), blk, blk, blk, blk],
        out_shape=[
            jax.ShapeDtypeStruct((d, t), BF16),
            jax.ShapeDtypeStruct(hk, BF16),
            jax.ShapeDtypeStruct(hk, BF16),
            jax.ShapeDtypeStruct(hk, F32),
            jax.ShapeDtypeStruct(hk, F32),
        ],
        scratch_shapes=[pltpu.VMEM((wall.shape[0], tm), F32),
                        pltpu.VMEM((PEER_HEADS, PEER_TOPK) + vreg, F32),
                        pltpu.VMEM((PEER_HEADS,) + vreg, F32),
                        pltpu.VMEM((PEER_TOPK,) + vreg, F32),
                        pltpu.VMEM((PEER_TOPK,) + vreg, F32),
                        pltpu.VMEM((PEER_TOPK,) + vreg, F32),
                        pltpu.VMEM(vreg, F32)],
        compiler_params=_params(("parallel",)),
        name="peer_route",
    )(h, wall)


def _peer_kernel(xt_ref, u_ref, vt_ref, rank2_ref, p2_ref, n1_ref, p1_ref, h_ref, g_ref, b_ref,
                 out_ref, acc_ref, y_ref, *, rows):
    e = pl.program_id(1)

    @pl.when(e == 0)
    def _():
        acc_ref[...] = jnp.zeros_like(acc_ref)

    xt = xt_ref[...]
    tm = xt.shape[1]
    pack = 2 * SUBLANES
    slabs = PEER_NKEYS // pack
    zero = jnp.zeros((), BF16)

    def row_tile(ref, hd, l):
        r8 = jnp.broadcast_to(ref[hd, l:l + 1, :], (SUBLANES, tm))
        return jnp.concatenate([r8, r8], axis=0).astype(BF16)[None]

    for l in range(rows):
        es = slice(l * PEER_NKEYS, (l + 1) * PEER_NKEYS)
        pre = _dot(u_ref[es, :], xt).astype(BF16)
        act = pre * (1.0 + lax.erf(pre * (1.0 / math.sqrt(2.0))))
        w = None
        for hd in range(PEER_HEADS):
            keep = rank2_ref[hd].reshape(slabs, pack, tm) < row_tile(n1_ref, hd, l)
            term = jnp.where(keep, p2_ref[hd].reshape(slabs, pack, tm), zero) * row_tile(p1_ref, hd, l)
            w = term if w is None else w + term
        y_ref[es, :] = act * w.reshape(PEER_NKEYS, tm)
    acc_ref[...] += _dot(vt_ref[...], y_ref[...])

    @pl.when(e == pl.num_programs(1) - 1)
    def _():
        ffn = jnp.transpose(acc_ref[...])
        out_ref[...] = _layer_norm(ALPHA * h_ref[...] + ffn, g_ref[...], b_ref[...])


def _peer(xt, u, vt, rank2, p2, n1, p1, h, g, b, tm, te):
    t, d = h.shape
    rows = te // PEER_NKEYS
    kern = functools.partial(_peer_kernel, rows=rows)
    route_blk = pl.BlockSpec((PEER_HEADS, PEER_NKEYS, tm), lambda i, e: (0, 0, i))
    row_blk = pl.BlockSpec((PEER_HEADS, rows, tm), lambda i, e: (0, e, i))
    return pl.pallas_call(
        kern,
        grid=(t // tm, PEER_EXPERTS // te),
        in_specs=[
            pl.BlockSpec((d, tm), lambda i, e: (0, i)),
            pl.BlockSpec((te, d), lambda i, e: (e, 0)),
            pl.BlockSpec((d, te), lambda i, e: (0, e)),
            route_blk, route_blk, row_blk, row_blk,
            pl.BlockSpec((tm, d), lambda i, e: (i, 0)),
            pl.BlockSpec((1, d), lambda i, e: (0, 0)),
            pl.BlockSpec((1, d), lambda i, e: (0, 0)),
        ],
        out_specs=pl.BlockSpec((tm, d), lambda i, e: (i, 0)),
        out_shape=jax.ShapeDtypeStruct((t, d), F32),
        scratch_shapes=[pltpu.VMEM((d, tm), F32), pltpu.VMEM((te, tm), BF16)],
        compiler_params=_params(("parallel", "arbitrary")),
        name="peer_dense",
    )(xt, u, vt, rank2, p2, n1, p1, h, g, b)


def _pack_w_in(w):
    q, k, v, r, a, su, sv = jnp.split(
        w, [256, 512, 1024, 1536, 1552, 2064], axis=-1)
    a = jnp.pad(a, ((0, 0), (0, A_PAD - GLA_GATE_RANK)))
    return jnp.concatenate([q, k, v, r, su, sv, a], axis=-1).astype(BF16)


def _forward(x, ln_in_g, ln_in_b, w_in, w_gate_up, b_gate, gla_norm_g, sgu_ln_g, sgu_ln_b,
             sgu_w, sgu_b, w_out, ln1_g, ln1_b, peer_wq, peer_k1, peer_k2, peer_u, peer_v,
             ln2_g, ln2_b, *, tm_proj, ts_gla, nb_gla, tm_mix, tm_route, tm_peer, te_peer):
    batch, seq, d = x.shape
    t = batch * seq
    row = lambda p: p.reshape(1, -1)
    h = None
    for l in range(DEPTH):
        if l == 0:
            h, p = _in_proj_ln(x.reshape(t, d), ln_in_g, ln_in_b, _pack_w_in(w_in[l]), tm_proj)
        else:
            p = _in_proj(h, _pack_w_in(w_in[l]), tm_proj)
        wg = jnp.pad(w_gate_up[l], ((0, A_PAD - GLA_GATE_RANK), (0, 0)))
        o = _gla(p.reshape(batch, seq, P_COLS), wg, row(b_gate[l]), row(gla_norm_g[l]),
                 batch, seq, ts_gla, nb_gla)
        bs_full = jnp.broadcast_to(sgu_b[l][:, :, None],
                                   (SGU_GROUPS, SGU_WINDOW, SGU_GROUP_DIM))
        h = _mix_out(p, o.reshape(t, GLA_WIDTH), h, row(sgu_ln_g[l]), row(sgu_ln_b[l]),
                     sgu_w[l], bs_full, w_out[l].astype(BF16), row(ln1_g[l]), row(ln1_b[l]),
                     tm_mix)
        wall = _fold_keys(peer_k1[l], peer_k2[l], peer_wq[l])
        xt, rank2, p2, n1, p1 = _route(h, wall, tm_route)
        h = _peer(xt, peer_u[l].astype(BF16), jnp.transpose(peer_v[l].astype(BF16)),
                  rank2, p2, n1, p1, h, row(ln2_g[l]), row(ln2_b[l]), tm_peer, te_peer)
    return h.reshape(batch, seq, d)


def kernel(x, ln_in_g, ln_in_b, w_in, w_gate_up, b_gate, gla_norm_g, sgu_ln_g, sgu_ln_b, sgu_w,
           sgu_b, w_out, ln1_g, ln1_b, peer_wq, peer_k1, peer_k2, peer_u, peer_v, ln2_g, ln2_b):
    return _forward(x, ln_in_g, ln_in_b, w_in, w_gate_up, b_gate, gla_norm_g, sgu_ln_g,
                    sgu_ln_b, sgu_w, sgu_b, w_out, ln1_g, ln1_b, peer_wq, peer_k1, peer_k2,
                    peer_u, peer_v, ln2_g, ln2_b,
                    tm_proj=512, ts_gla=256, nb_gla=8, tm_mix=512, tm_route=512, tm_peer=1024,
                    te_peer=1024)
```

```python
import functools
import math

import jax
import jax.numpy as jnp
from jax import lax
from jax.experimental import pallas as pl
from jax.experimental.pallas import tpu as pltpu

F32 = jnp.float32
BF16 = jnp.bfloat16

D_MODEL = 1024
DEPTH = 2
CHUNK = 64
GLA_HEADS = 4
GLA_DK = 64
GLA_DV = 128
GLA_WIDTH = GLA_HEADS * GLA_DV
GLA_QK = GLA_HEADS * GLA_DK
GLA_GATE_RANK = 16
GLA_GATE_NORMALIZER = 16.0
SGU_WIDTH = 512
SGU_GROUPS = 4
SGU_GROUP_DIM = 128
SGU_WINDOW = 128
PEER_HEADS = 8
PEER_NKEYS = 128
PEER_EXPERTS = PEER_NKEYS * PEER_NKEYS
PEER_TOPK = 16
PEER_DQ = 256
PEER_DQ_HALF = 128
LN_EPS = 1e-5
ALPHA = (2.0 * DEPTH) ** 0.25

LANES = 128
SUBLANES = 8

COL_Q = 0
COL_K = COL_Q + GLA_QK
COL_V = COL_K + GLA_QK
COL_R = COL_V + GLA_WIDTH
COL_SU = COL_R + GLA_WIDTH
COL_SV = COL_SU + SGU_WIDTH
COL_A = COL_SV + SGU_WIDTH
A_PAD = LANES
P_COLS = COL_A + A_PAD

SUB = 16
EXP_CLAMP = 80.0

V7X_VMEM_BYTES = 64 * 1024 * 1024
VMEM_LIMIT = V7X_VMEM_BYTES * 7 // 8


def _params(sem):
    return pltpu.CompilerParams(dimension_semantics=sem, vmem_limit_bytes=VMEM_LIMIT)


def _layer_norm(x, g, b):
    mu = jnp.mean(x, axis=-1, keepdims=True)
    xc = x - mu
    var = jnp.mean(xc * xc, axis=-1, keepdims=True)
    return xc * lax.rsqrt(var + LN_EPS) * g + b


def _gelu(x):
    return 0.5 * x * (1.0 + lax.erf(x * (1.0 / math.sqrt(2.0))))


def _dot(a, b):
    return jnp.dot(a, b, preferred_element_type=F32)


def _dot_nt(a, b):
    return lax.dot_general(a, b, (((1,), (1,)), ((), ())), preferred_element_type=F32)


def _dot_tn(a, b):
    return lax.dot_general(a, b, (((0,), (0,)), ((), ())), preferred_element_type=F32)


def _in_proj_ln_kernel(x_ref, g_ref, b_ref, w_ref, h_ref, o_ref):
    h = _layer_norm(x_ref[...], g_ref[...], b_ref[...])
    h_ref[...] = h
    o_ref[...] = _dot(h.astype(BF16), w_ref[...])


def _in_proj_ln(x, g, b, w, tm):
    t, d = x.shape
    n = w.shape[1]
    return pl.pallas_call(
        _in_proj_ln_kernel,
        grid=(t // tm,),
        in_specs=[pl.BlockSpec((tm, d), lambda i: (i, 0)),
                  pl.BlockSpec((1, d), lambda i: (0, 0)),
                  pl.BlockSpec((1, d), lambda i: (0, 0)),
                  pl.BlockSpec((d, n), lambda i: (0, 0))],
        out_specs=[pl.BlockSpec((tm, d), lambda i: (i, 0)),
                   pl.BlockSpec((tm, n), lambda i: (i, 0))],
        out_shape=[jax.ShapeDtypeStruct((t, d), F32), jax.ShapeDtypeStruct((t, n), F32)],
        compiler_params=_params(("parallel",)),
        name="in_proj_ln",
    )(x, g.reshape(1, d), b.reshape(1, d), w)


def _in_proj_kernel(h_ref, w_ref, o_ref):
    o_ref[...] = _dot(h_ref[...].astype(BF16), w_ref[...])


def _in_proj(h, w, tm):
    t, d = h.shape
    n = w.shape[1]
    return pl.pallas_call(
        _in_proj_kernel,
        grid=(t // tm,),
        in_specs=[pl.BlockSpec((tm, d), lambda i: (i, 0)),
                  pl.BlockSpec((d, n), lambda i: (0, 0))],
        out_specs=pl.BlockSpec((tm, n), lambda i: (i, 0)),
        out_shape=jax.ShapeDtypeStruct((t, n), F32),
        compiler_params=_params(("parallel",)),
        name="in_proj",
    )(h, w)


def _gla_kernel(q_ref, k_ref, v_ref, r_ref, a_ref, wg_ref, bg_ref, ng_ref, o_ref, st_ref,
                *, chunks, nb):
    @pl.when(pl.program_id(1) == 0)
    def _():
        st_ref[...] = jnp.zeros_like(st_ref)

    ts = chunks * CHUNK
    row = lax.broadcasted_iota(jnp.int32, (CHUNK, CHUNK), 0)
    col = lax.broadcasted_iota(jnp.int32, (CHUNK, CHUNK), 1)
    causal = col <= row
    trow = lax.broadcasted_iota(jnp.int32, (ts, ts), 0)
    tcol = lax.broadcasted_iota(jnp.int32, (ts, ts), 1)
    tril = jnp.logical_and(tcol <= trow, tcol // CHUNK == trow // CHUNK).astype(F32)
    lane_head = lax.broadcasted_iota(jnp.int32, (1, LANES), 1) // GLA_DK
    ng = ng_ref[...]
    groups = GLA_QK // LANES
    per_group = LANES // GLA_DK
    items = [(bi, c) for bi in range(nb) for c in range(chunks)]

    bc = []
    for bi in range(nb):
        z = jnp.dot(a_ref[bi], wg_ref[...], preferred_element_type=F32,
                    precision=lax.Precision.HIGHEST) + bg_ref[...]
        logg = (jnp.minimum(z, 0.0) - jnp.log1p(jnp.exp(-jnp.abs(z)))) * (1.0 / GLA_GATE_NORMALIZER)
        bc.append(jnp.dot(tril, logg, preferred_element_type=F32, precision=lax.Precision.HIGHEST))

    scores = {}
    for bi, c in items:
        rows = slice(c * CHUNK, (c + 1) * CHUNK)
        for grp in range(groups):
            ls = slice(grp * LANES, (grp + 1) * LANES)
            qg = q_ref[bi, rows, ls] * (GLA_DK ** -0.5)
            kg = k_ref[bi, rows, ls]
            bg = bc[bi][rows, ls]
            parts = []
            for i in range(CHUNK // SUB):
                lo, hi = i * SUB, (i + 1) * SUB
                ref = bg[lo:lo + 1, :]
                qi = qg[lo:hi, :] * jnp.exp(bg[lo:hi, :] - ref)
                ki = (kg * jnp.exp(jnp.minimum(ref - bg, EXP_CLAMP))).astype(BF16)
                q2 = jnp.concatenate([jnp.where(lane_head == 0, qi, 0.0),
                                      jnp.where(lane_head == 1, qi, 0.0)], axis=0).astype(BF16)
                parts.append(_dot_nt(q2, ki))
            scores[bi, c, grp] = parts

    intra = {}
    for bi, c in items:
        rows = slice(c * CHUNK, (c + 1) * CHUNK)
        for h in range(GLA_HEADS):
            grp, sub = divmod(h, per_group)
            att = jnp.concatenate([p[sub * SUB:(sub + 1) * SUB, :] for p in scores[bi, c, grp]],
                                  axis=0)
            a_h = jnp.where(causal, att, 0.0).astype(BF16)
            vh_b = v_ref[bi, rows, h * GLA_DV:(h + 1) * GLA_DV].astype(BF16)
            intra[bi, c, h] = _dot(a_h, vh_b)

    for c in range(chunks):
        rows = slice(c * CHUNK, (c + 1) * CHUNK)
        for bi in range(nb):
            for grp in range(groups):
                ls = slice(grp * LANES, (grp + 1) * LANES)
                qg = q_ref[bi, rows, ls] * (GLA_DK ** -0.5)
                kg = k_ref[bi, rows, ls]
                bg = bc[bi][rows, ls]
                b_last = bg[CHUNK - 1:CHUNK, :]
                q_inter = (qg * jnp.exp(bg)).astype(BF16)
                kd = kg * jnp.exp(b_last - bg)
                decay = jnp.exp(b_last)
                for sub in range(per_group):
                    h = grp * per_group + sub
                    vs = slice(h * GLA_DV, (h + 1) * GLA_DV)
                    vh_b = v_ref[bi, rows, vs].astype(BF16)
                    state_t = st_ref[bi, h]
                    o = _dot_nt(q_inter, state_t.astype(BF16)) + intra[bi, c, h]
                    kd_h = jnp.where(lane_head == sub, kd, 0.0).astype(BF16)
                    st_ref[bi, h] = state_t * decay + _dot_tn(vh_b, kd_h)
                    ms = jnp.mean(o * o, axis=-1, keepdims=True)
                    on = o * lax.rsqrt(ms + LN_EPS) * ng
                    rh = r_ref[bi, rows, vs]
                    o_ref[bi, rows, vs] = on * (rh * (1.0 / (1.0 + jnp.exp(-rh))))


def _gla(p, wg, bg, ng, batch, seq, ts, nb):
    kern = functools.partial(_gla_kernel, chunks=ts // CHUNK, nb=nb)
    return pl.pallas_call(
        kern,
        grid=(batch // nb, seq // ts),
        in_specs=[
            pl.BlockSpec((nb, ts, GLA_QK), lambda b, s: (b, s, COL_Q // GLA_QK)),
            pl.BlockSpec((nb, ts, GLA_QK), lambda b, s: (b, s, COL_K // GLA_QK)),
            pl.BlockSpec((nb, ts, GLA_WIDTH), lambda b, s: (b, s, COL_V // GLA_WIDTH)),
            pl.BlockSpec((nb, ts, GLA_WIDTH), lambda b, s: (b, s, COL_R // GLA_WIDTH)),
            pl.BlockSpec((nb, ts, A_PAD), lambda b, s: (b, s, COL_A // A_PAD)),
            pl.BlockSpec((A_PAD, GLA_QK), lambda b, s: (0, 0)),
            pl.BlockSpec((1, GLA_QK), lambda b, s: (0, 0)),
            pl.BlockSpec((1, GLA_DV), lambda b, s: (0, 0)),
        ],
        out_specs=pl.BlockSpec((nb, ts, GLA_WIDTH), lambda b, s: (b, s, 0)),
        out_shape=jax.ShapeDtypeStruct((batch, seq, GLA_WIDTH), F32),
        scratch_shapes=[pltpu.VMEM((nb, GLA_HEADS, GLA_DV, LANES), F32)],
        compiler_params=_params(("parallel", "arbitrary")),
        name="gla",
    )(p, p, p, p, p, wg, bg, ng)


def _mix_out_kernel(su_ref, sv_ref, o_ref, h_ref, lg_ref, lb_ref, ws_ref, bs_ref, wo_ref,
                    g1_ref, b1_ref, out_ref, gate_ref, *, windows):
    u = _gelu(su_ref[...])
    v = _layer_norm(_gelu(sv_ref[...]), lg_ref[...], lb_ref[...])
    blk_r = lax.broadcasted_iota(jnp.int32, (SGU_WINDOW, SGU_WINDOW), 0) // CHUNK
    blk_c = lax.broadcasted_iota(jnp.int32, (SGU_WINDOW, SGU_WINDOW), 1) // CHUNK
    keep = blk_c <= blk_r
    for g in range(SGU_GROUPS):
        cs = slice(g * SGU_GROUP_DIM, (g + 1) * SGU_GROUP_DIM)
        w = jnp.where(keep, ws_ref[g], 0.0).astype(BF16)
        bias = bs_ref[g]
        for n in range(windows):
            rs = slice(n * SGU_WINDOW, (n + 1) * SGU_WINDOW)
            sv = _dot(w, v[rs, cs].astype(BF16)) + bias
            gate_ref[rs, cs] = u[rs, cs] * sv
    mix = _dot(o_ref[...].astype(BF16), wo_ref[0:GLA_WIDTH, :])
    mix = mix + _dot(gate_ref[...].astype(BF16), wo_ref[GLA_WIDTH:, :])
    out_ref[...] = _layer_norm(ALPHA * h_ref[...] + mix, g1_ref[...], b1_ref[...])


def _mix_out(p, o, h, lg, lb, ws, bs_full, wo, g1, b1, tm):
    t, d = h.shape
    kern = functools.partial(_mix_out_kernel, windows=tm // SGU_WINDOW)
    const2 = lambda i: (0, 0)
    const3 = lambda i: (0, 0, 0)
    return pl.pallas_call(
        kern,
        grid=(t // tm,),
        in_specs=[
            pl.BlockSpec((tm, SGU_WIDTH), lambda i: (i, COL_SU // SGU_WIDTH)),
            pl.BlockSpec((tm, SGU_WIDTH), lambda i: (i, COL_SV // SGU_WIDTH)),
            pl.BlockSpec((tm, GLA_WIDTH), lambda i: (i, 0)),
            pl.BlockSpec((tm, d), lambda i: (i, 0)),
            pl.BlockSpec((1, SGU_WIDTH), const2),
            pl.BlockSpec((1, SGU_WIDTH), const2),
            pl.BlockSpec((SGU_GROUPS, SGU_WINDOW, SGU_WINDOW), const3),
            pl.BlockSpec((SGU_GROUPS, SGU_WINDOW, SGU_GROUP_DIM), const3),
            pl.BlockSpec((d, d), const2),
            pl.BlockSpec((1, d), const2),
            pl.BlockSpec((1, d), const2),
        ],
        out_specs=pl.BlockSpec((tm, d), lambda i: (i, 0)),
        out_shape=jax.ShapeDtypeStruct((t, d), F32),
        scratch_shapes=[pltpu.VMEM((tm, SGU_WIDTH), F32)],
        compiler_params=_params(("parallel",)),
        name="mix_out",
    )(p, p, o, h, lg, lb, ws, bs_full, wo, g1, b1)


def _sort_network(n):
    pairs = []

    def merge(lo, hi, r):
        step = r * 2
        if step < hi - lo:
            merge(lo, hi, step)
            merge(lo + r, hi, step)
            pairs.extend((i, i + r) for i in range(lo + r, hi - r, step))
        else:
            pairs.append((lo, lo + r))

    def sort(lo, hi):
        if hi - lo >= 1:
            mid = lo + (hi - lo) // 2
            sort(lo, mid)
            sort(mid + 1, hi)
            merge(lo, hi, 1)

    sort(0, n - 1)
    return pairs


def _ce(x, y):
    if y is None:
        return x, None
    if x is None:
        return y, None
    return jnp.maximum(x, y), jnp.minimum(x, y)


def _merge_top(a, b):
    n = len(a)
    v = [_ce(a[k], b[n - 1 - k])[0] for k in range(n)]
    d = n // 2
    while d >= 1:
        for i in range(n):
            if i & d == 0:
                v[i], v[i + d] = _ce(v[i], v[i + d])
        d //= 2
    return v


def _top_sorted(tiles):
    v = list(tiles)
    for i, j in _sort_network(len(v)):
        v[i], v[j] = _ce(v[i], v[j])
    shift = SUBLANES // 2
    while shift >= 1:
        v = _merge_top(v, [pltpu.roll(x, shift, axis=0) for x in v])
        shift //= 2
    return v


def _pad(vals):
    return list(vals) + [None] * (PEER_TOPK - len(vals))


def _fold_kernel(k_ref, wq_ref, o_ref):
    o_ref[...] = lax.dot_general(k_ref[...], wq_ref[...], (((1,), (1,)), ((), ())),
                                 preferred_element_type=F32,
                                 precision=lax.Precision.HIGHEST).astype(BF16)


def _fold_keys(k1, k2, wq):
    d = wq.shape[0]
    keys = jnp.stack([k1, k2])
    return pl.pallas_call(
        _fold_kernel,
        grid=(PEER_HEADS, 2),
        in_specs=[pl.BlockSpec((None, PEER_NKEYS, PEER_DQ_HALF), lambda h, s: (s, 0, 0)),
                  pl.BlockSpec((d, PEER_DQ_HALF), lambda h, s: (0, 2 * h + s))],
        out_specs=pl.BlockSpec((PEER_NKEYS, d), lambda h, s: (2 * h + s, 0)),
        out_shape=jax.ShapeDtypeStruct((PEER_HEADS * 2 * PEER_NKEYS, d), BF16),
        compiler_params=_params(("parallel", "parallel")),
        name="peer_fold",
    )(keys, wq)


def _route_kernel(h_ref, wall_ref, xt_ref, rank2_ref, p2_ref, n1_ref, p1_ref,
                  s_ref, b_ref, a0_ref, pa_ref, pb_ref, cut_ref, extra_ref, invz_ref):
    tm = h_ref.shape[0]
    xt = jnp.transpose(h_ref[...]).astype(BF16)
    xt_ref[...] = xt
    s_ref[...] = _dot(wall_ref[...], xt)
    ntile = PEER_NKEYS // SUBLANES

    for c in range(tm // LANES):
        ls = slice(c * LANES, (c + 1) * LANES)

        def tile(hd, half, k):
            r0 = pl.multiple_of(hd * (2 * PEER_NKEYS) + half * PEER_NKEYS + k * SUBLANES, SUBLANES)
            return s_ref[pl.ds(r0, SUBLANES), ls]

        def sort_body(hd, carry):
            a = _top_sorted([tile(hd, 0, k) for k in range(ntile)])
            b = _top_sorted([tile(hd, 1, k) for k in range(ntile)])
            a0_ref[hd] = a[0]
            for r in range(PEER_TOPK):
                b_ref[hd, r] = b[r]
                pa_ref[r, pl.ds(hd, 1), :] = a[r][0:1, :]
                pb_ref[r, pl.ds(hd, 1), :] = b[r][0:1, :]
            return carry

        lax.fori_loop(0, PEER_HEADS, sort_body, 0)

        a = [pa_ref[r] for r in range(PEER_TOPK)]
        b = [pb_ref[r] for r in range(PEER_TOPK)]
        row = [_pad([a[i] + b[j] for j in range(PEER_TOPK // (i + 1))]) for i in range(SUBLANES)]
        tail = _pad([a[i] + b[0] for i in range(SUBLANES, PEER_TOPK)])
        small = _merge_top(_merge_top(row[4], row[5]), _merge_top(row[6], row[7]))
        mid = _merge_top(_merge_top(row[2], row[3]), small)
        top = _merge_top(row[0], _merge_top(_merge_top(row[1], tail), mid))
        thr = top[PEER_TOPK - 1]
        z = None
        for r in range(PEER_TOPK):
            e = jnp.exp(top[r] - top[0])
            z = e if z is None else z + e
        invz_ref[...] = 1.0 / z
        for j in range(SUBLANES):
            cj = jnp.full_like(thr, float("inf"))
            for i in range(min(PEER_TOPK // (j + 1), SUBLANES)):
                cj = jnp.where(row[i][j] >= thr, a[i], cj)
            if j == 0:
                for i in range(SUBLANES, PEER_TOPK):
                    cj = jnp.where(tail[i - SUBLANES] >= thr, a[i], cj)
            cut_ref[j] = cj
        extra = jnp.zeros_like(thr)
        for j in range(SUBLANES, PEER_TOPK):
            extra = extra + jnp.where(row[0][j] >= thr, 1.0, 0.0)
        extra_ref[...] = extra

        def emit_body(hd, carry):
            head_row = lambda ref, *idx: jnp.broadcast_to(ref[(*idx, pl.ds(hd, 1), slice(None))],
                                                          (SUBLANES, LANES))
            cut = [head_row(cut_ref, j) for j in range(SUBLANES)]
            extra = head_row(extra_ref)
            inv_z = head_row(invz_ref)
            a0 = a0_ref[hd]
            bs = [b_ref[hd, r] for r in range(PEER_TOPK)]
            for kp in range(ntile // 2):
                r2, pp2 = [], []
                for k in (2 * kp, 2 * kp + 1):
                    rows = slice(k * SUBLANES, (k + 1) * SUBLANES)
                    s1k = tile(hd, 0, k)
                    n1 = jnp.zeros_like(s1k)
                    for j in range(SUBLANES):
                        n1 = jnp.where(s1k >= cut[j], float(j + 1), n1)
                    n1_ref[hd, rows, ls] = jnp.where(s1k >= a0, n1 + extra, n1)
                    p1_ref[hd, rows, ls] = jnp.exp(s1k - a0) * inv_z
                    s2k = tile(hd, 1, k)
                    rk = jnp.full_like(s2k, float(PEER_TOPK))
                    for r in range(PEER_TOPK - 1, -1, -1):
                        rk = jnp.where(s2k >= bs[r], float(r), rk)
                    r2.append(rk)
                    pp2.append(jnp.exp(s2k - bs[0]))
                rows2 = slice(kp * 2 * SUBLANES, (kp + 1) * 2 * SUBLANES)
                rank2_ref[hd, rows2, ls] = jnp.concatenate(r2, axis=0).astype(BF16)
                p2_ref[hd, rows2, ls] = jnp.concatenate(pp2, axis=0).astype(BF16)
            return carry

        lax.fori_loop(0, PEER_HEADS, emit_body, 0)


def _route(h, wall, tm):
    t, d = h.shape
    hk = (PEER_HEADS, PEER_NKEYS, t)
    blk = pl.BlockSpec((PEER_HEADS, PEER_NKEYS, tm), lambda i: (0, 0, i))
    vreg = (SUBLANES, LANES)
    return pl.pallas_call(
        _route_kernel,
        grid=(t // tm,),
        in_specs=[
            pl.BlockSpec((tm, d), lambda i: (i, 0)),
            pl.BlockSpec(wall.shape, lambda i: (0, 0)),
        ],
        out_specs=[pl.BlockSpec((d, tm), lambda i: (0, i)), blk, blk, blk, blk],
        out_shape=[
            jax.ShapeDtypeStruct((d, t), BF16),
            jax.ShapeDtypeStruct(hk, BF16),
            jax.ShapeDtypeStruct(hk, BF16),
            jax.ShapeDtypeStruct(hk, F32),
            jax.ShapeDtypeStruct(hk, F32),
        ],
        scratch_shapes=[pltpu.VMEM((wall.shape[0], tm), F32),
                        pltpu.VMEM((PEER_HEADS, PEER_TOPK) + vreg, F32),
                        pltpu.VMEM((PEER_HEADS,) + vreg, F32),
                        pltpu.VMEM((PEER_TOPK,) + vreg, F32),
                        pltpu.VMEM((PEER_TOPK,) + vreg, F32),
                        pltpu.VMEM((SUBLANES,) + vreg, F32),
                        pltpu.VMEM(vreg, F32),
                        pltpu.VMEM(vreg, F32)],
        compiler_params=_params(("parallel",)),
        name="peer_route",
    )(h, wall)


def _peer_kernel(xt_ref, u_ref, vt_ref, rank2_ref, p2_ref, n1_ref, p1_ref, h_ref, g_ref, b_ref,
                 out_ref, acc_ref, y_ref, *, rows):
    e = pl.program_id(1)

    @pl.when(e == 0)
    def _():
        acc_ref[...] = jnp.zeros_like(acc_ref)

    xt = xt_ref[...]
    tm = xt.shape[1]
    pack = 2 * SUBLANES
    slabs = PEER_NKEYS // pack
    zero = jnp.zeros((), BF16)

    def row_tile(ref, hd, l):
        r8 = jnp.broadcast_to(ref[hd, l:l + 1, :], (SUBLANES, tm))
        return jnp.concatenate([r8, r8], axis=0).astype(BF16)[None]

    for l in range(rows):
        es = slice(l * PEER_NKEYS, (l + 1) * PEER_NKEYS)
        act = _gelu(_dot(u_ref[es, :], xt).astype(BF16))
        w = None
        for hd in range(PEER_HEADS):
            keep = rank2_ref[hd].reshape(slabs, pack, tm) < row_tile(n1_ref, hd, l)
            term = jnp.where(keep, p2_ref[hd].reshape(slabs, pack, tm), zero) * row_tile(p1_ref, hd, l)
            w = term if w is None else w + term
        y_ref[es, :] = act * w.reshape(PEER_NKEYS, tm)
    acc_ref[...] += _dot(vt_ref[...], y_ref[...])

    @pl.when(e == pl.num_programs(1) - 1)
    def _():
        ffn = jnp.transpose(acc_ref[...])
        out_ref[...] = _layer_norm(ALPHA * h_ref[...] + ffn, g_ref[...], b_ref[...])


def _peer(xt, u, vt, rank2, p2, n1, p1, h, g, b, tm, te):
    t, d = h.shape
    rows = te // PEER_NKEYS
    kern = functools.partial(_peer_kernel, rows=rows)
    route_blk = pl.BlockSpec((PEER_HEADS, PEER_NKEYS, tm), lambda i, e: (0, 0, i))
    row_blk = pl.BlockSpec((PEER_HEADS, rows, tm), lambda i, e: (0, e, i))
    return pl.pallas_call(
        kern,
        grid=(t // tm, PEER_EXPERTS // te),
        in_specs=[
            pl.BlockSpec((d, tm), lambda i, e: (0, i)),
            pl.BlockSpec((te, d), lambda i, e: (e, 0)),
            pl.BlockSpec((d, te), lambda i, e: (0, e)),
            route_blk, route_blk, row_blk, row_blk,
            pl.BlockSpec((tm, d), lambda i, e: (i, 0)),
            pl.BlockSpec((1, d), lambda i, e: (0, 0)),
            pl.BlockSpec((1, d), lambda i, e: (0, 0)),
        ],
        out_specs=pl.BlockSpec((tm, d), lambda i, e: (i, 0)),
        out_shape=jax.ShapeDtypeStruct((t, d), F32),
        scratch_shapes=[pltpu.VMEM((d, tm), F32), pltpu.VMEM((te, tm), BF16)],
        compiler_params=_params(("parallel", "arbitrary")),
        name="peer_dense",
    )(xt, u, vt, rank2, p2, n1, p1, h, g, b)


def _pack_w_in(w):
    q, k, v, r, a, su, sv = jnp.split(
        w, [256, 512, 1024, 1536, 1552, 2064], axis=-1)
    a = jnp.pad(a, ((0, 0), (0, A_PAD - GLA_GATE_RANK)))
    return jnp.concatenate([q, k, v, r, su, sv, a], axis=-1).astype(BF16)


def _forward(x, ln_in_g, ln_in_b, w_in, w_gate_up, b_gate, gla_norm_g, sgu_ln_g, sgu_ln_b,
             sgu_w, sgu_b, w_out, ln1_g, ln1_b, peer_wq, peer_k1, peer_k2, peer_u, peer_v,
             ln2_g, ln2_b, *, tm_proj, ts_gla, nb_gla, tm_mix, tm_route, tm_peer, te_peer):
    batch, seq, d = x.shape
    t = batch * seq
    row = lambda p: p.reshape(1, -1)
    h = None
    for l in range(DEPTH):
        if l == 0:
            h, p = _in_proj_ln(x.reshape(t, d), ln_in_g, ln_in_b, _pack_w_in(w_in[l]), tm_proj)
        else:
            p = _in_proj(h, _pack_w_in(w_in[l]), tm_proj)
        wg = jnp.pad(w_gate_up[l], ((0, A_PAD - GLA_GATE_RANK), (0, 0)))
        o = _gla(p.reshape(batch, seq, P_COLS), wg, row(b_gate[l]), row(gla_norm_g[l]),
                 batch, seq, ts_gla, nb_gla)
        bs_full = jnp.broadcast_to(sgu_b[l][:, :, None],
                                   (SGU_GROUPS, SGU_WINDOW, SGU_GROUP_DIM))
        h = _mix_out(p, o.reshape(t, GLA_WIDTH), h, row(sgu_ln_g[l]), row(sgu_ln_b[l]),
                     sgu_w[l], bs_full, w_out[l].astype(BF16), row(ln1_g[l]), row(ln1_b[l]),
                     tm_mix)
        wall = _fold_keys(peer_k1[l], peer_k2[l], peer_wq[l])
        xt, rank2, p2, n1, p1 = _route(h, wall, tm_route)
        h = _peer(xt, peer_u[l].astype(BF16), jnp.transpose(peer_v[l].astype(BF16)),
                  rank2, p2, n1, p1, h, row(ln2_g[l]), row(ln2_b[l]), tm_peer, te_peer)
    return h.reshape(batch, seq, d)


def kernel(x, ln_in_g, ln_in_b, w_in, w_gate_up, b_gate, gla_norm_g, sgu_ln_g, sgu_ln_b, sgu_w,
           sgu_b, w_out, ln1_g, ln1_b, peer_wq, peer_k1, peer_k2, peer_u, peer_v, ln2_g, ln2_b):
    return _forward(x, ln_in_g, ln_in_b, w_in, w_gate_up, b_gate, gla_norm_g, sgu_ln_g,
                    sgu_ln_b, sgu_w, sgu_b, w_out, ln1_g, ln1_b, peer_wq, peer_k1, peer_k2,
                    peer_u, peer_v, ln2_g, ln2_b,
                    tm_proj=512, ts_gla=256, nb_gla=8, tm_mix=512, tm_route=512, tm_peer=512,
                    te_peer=2048)
```

```python
import functools
import math

import jax
import jax.numpy as jnp
from jax import lax
from jax.experimental import pallas as pl
from jax.experimental.pallas import tpu as pltpu

F32 = jnp.float32
BF16 = jnp.bfloat16

D_MODEL = 1024
DEPTH = 2
CHUNK = 64
GLA_HEADS = 4
GLA_DK = 64
GLA_DV = 128
GLA_WIDTH = GLA_HEADS * GLA_DV
GLA_QK = GLA_HEADS * GLA_DK
GLA_GATE_RANK = 16
GLA_GATE_NORMALIZER = 16.0
SGU_WIDTH = 512
SGU_GROUPS = 4
SGU_GROUP_DIM = 128
SGU_WINDOW = 128
PEER_HEADS = 8
PEER_NKEYS = 128
PEER_EXPERTS = PEER_NKEYS * PEER_NKEYS
PEER_TOPK = 16
PEER_DQ = 256
PEER_DQ_HALF = 128
LN_EPS = 1e-5
ALPHA = (2.0 * DEPTH) ** 0.25

LANES = 128
SUBLANES = 8

COL_Q = 0
COL_K = COL_Q + GLA_QK
COL_V = COL_K + GLA_QK
COL_R = COL_V + GLA_WIDTH
COL_SU = COL_R + GLA_WIDTH
COL_SV = COL_SU + SGU_WIDTH
COL_A = COL_SV + SGU_WIDTH
A_PAD = LANES
P_COLS = COL_A + A_PAD

SUB = 16
EXP_CLAMP = 80.0

V7X_VMEM_BYTES = 64 * 1024 * 1024
VMEM_LIMIT = V7X_VMEM_BYTES * 15 // 16


def _params(sem):
    return pltpu.CompilerParams(dimension_semantics=sem, vmem_limit_bytes=VMEM_LIMIT)


def _layer_norm(x, g, b):
    mu = jnp.mean(x, axis=-1, keepdims=True)
    xc = x - mu
    var = jnp.mean(xc * xc, axis=-1, keepdims=True)
    return xc * lax.rsqrt(var + LN_EPS) * g + b


def _gelu(x):
    return 0.5 * x * (1.0 + lax.erf(x * (1.0 / math.sqrt(2.0))))


def _dot(a, b):
    return jnp.dot(a, b, preferred_element_type=F32)


def _dot_nt(a, b):
    return lax.dot_general(a, b, (((1,), (1,)), ((), ())), preferred_element_type=F32)


def _dot_tn(a, b):
    return lax.dot_general(a, b, (((0,), (0,)), ((), ())), preferred_element_type=F32)


def _in_proj_ln_kernel(x_ref, g_ref, b_ref, w_ref, h_ref, o_ref):
    h = _layer_norm(x_ref[...], g_ref[...], b_ref[...])
    h_ref[...] = h
    o_ref[...] = _dot(h.astype(BF16), w_ref[...])


def _in_proj_ln(x, g, b, w, tm):
    t, d = x.shape
    n = w.shape[1]
    return pl.pallas_call(
        _in_proj_ln_kernel,
        grid=(t // tm,),
        in_specs=[pl.BlockSpec((tm, d), lambda i: (i, 0)),
                  pl.BlockSpec((1, d), lambda i: (0, 0)),
                  pl.BlockSpec((1, d), lambda i: (0, 0)),
                  pl.BlockSpec((d, n), lambda i: (0, 0))],
        out_specs=[pl.BlockSpec((tm, d), lambda i: (i, 0)),
                   pl.BlockSpec((tm, n), lambda i: (i, 0))],
        out_shape=[jax.ShapeDtypeStruct((t, d), F32), jax.ShapeDtypeStruct((t, n), F32)],
        compiler_params=_params(("parallel",)),
        name="in_proj_ln",
    )(x, g.reshape(1, d), b.reshape(1, d), w)


def _in_proj_kernel(h_ref, w_ref, o_ref):
    o_ref[...] = _dot(h_ref[...].astype(BF16), w_ref[...])


def _in_proj(h, w, tm):
    t, d = h.shape
    n = w.shape[1]
    return pl.pallas_call(
        _in_proj_kernel,
        grid=(t // tm,),
        in_specs=[pl.BlockSpec((tm, d), lambda i: (i, 0)),
                  pl.BlockSpec((d, n), lambda i: (0, 0))],
        out_specs=pl.BlockSpec((tm, n), lambda i: (i, 0)),
        out_shape=jax.ShapeDtypeStruct((t, n), F32),
        compiler_params=_params(("parallel",)),
        name="in_proj",
    )(h, w)


def _gla_kernel(q_ref, k_ref, v_ref, r_ref, a_ref, wg_ref, bg_ref, ng_ref, o_ref, st_ref,
                *, chunks, nb):
    @pl.when(pl.program_id(1) == 0)
    def _():
        st_ref[...] = jnp.zeros_like(st_ref)

    ts = chunks * CHUNK
    row = lax.broadcasted_iota(jnp.int32, (CHUNK, CHUNK), 0)
    col = lax.broadcasted_iota(jnp.int32, (CHUNK, CHUNK), 1)
    causal = col <= row
    trow = lax.broadcasted_iota(jnp.int32, (ts, ts), 0)
    tcol = lax.broadcasted_iota(jnp.int32, (ts, ts), 1)
    tril = jnp.logical_and(tcol <= trow, tcol // CHUNK == trow // CHUNK).astype(F32)
    lane_head = lax.broadcasted_iota(jnp.int32, (1, LANES), 1) // GLA_DK
    ng = ng_ref[...]
    groups = GLA_QK // LANES
    per_group = LANES // GLA_DK
    items = [(bi, c) for bi in range(nb) for c in range(chunks)]

    bc = []
    for bi in range(nb):
        z = jnp.dot(a_ref[bi], wg_ref[...], preferred_element_type=F32,
                    precision=lax.Precision.HIGHEST) + bg_ref[...]
        logg = (jnp.minimum(z, 0.0) - jnp.log1p(jnp.exp(-jnp.abs(z)))) * (1.0 / GLA_GATE_NORMALIZER)
        bc.append(jnp.dot(tril, logg, preferred_element_type=F32, precision=lax.Precision.HIGHEST))

    scores = {}
    for bi, c in items:
        rows = slice(c * CHUNK, (c + 1) * CHUNK)
        for grp in range(groups):
            ls = slice(grp * LANES, (grp + 1) * LANES)
            qg = q_ref[bi, rows, ls] * (GLA_DK ** -0.5)
            kg = k_ref[bi, rows, ls]
            bg = bc[bi][rows, ls]
            parts = []
            for i in range(CHUNK // SUB):
                lo, hi = i * SUB, (i + 1) * SUB
                ref = bg[lo:lo + 1, :]
                qi = qg[lo:hi, :] * jnp.exp(bg[lo:hi, :] - ref)
                ki = (kg * jnp.exp(jnp.minimum(ref - bg, EXP_CLAMP))).astype(BF16)
                q2 = jnp.concatenate([jnp.where(lane_head == 0, qi, 0.0),
                                      jnp.where(lane_head == 1, qi, 0.0)], axis=0).astype(BF16)
                parts.append(_dot_nt(q2, ki))
            scores[bi, c, grp] = parts

    intra = {}
    for bi, c in items:
        rows = slice(c * CHUNK, (c + 1) * CHUNK)
        for h in range(GLA_HEADS):
            grp, sub = divmod(h, per_group)
            att = jnp.concatenate([p[sub * SUB:(sub + 1) * SUB, :] for p in scores[bi, c, grp]],
                                  axis=0)
            a_h = jnp.where(causal, att, 0.0).astype(BF16)
            vh_b = v_ref[bi, rows, h * GLA_DV:(h + 1) * GLA_DV].astype(BF16)
            intra[bi, c, h] = _dot(a_h, vh_b)

    for c in range(chunks):
        rows = slice(c * CHUNK, (c + 1) * CHUNK)
        for bi in range(nb):
            for grp in range(groups):
                ls = slice(grp * LANES, (grp + 1) * LANES)
                qg = q_ref[bi, rows, ls] * (GLA_DK ** -0.5)
                kg = k_ref[bi, rows, ls]
                bg = bc[bi][rows, ls]
                b_last = bg[CHUNK - 1:CHUNK, :]
                q_inter = (qg * jnp.exp(bg)).astype(BF16)
                kd = kg * jnp.exp(b_last - bg)
                decay = jnp.exp(b_last)
                for sub in range(per_group):
                    h = grp * per_group + sub
                    vs = slice(h * GLA_DV, (h + 1) * GLA_DV)
                    vh_b = v_ref[bi, rows, vs].astype(BF16)
                    state_t = st_ref[bi, h]
                    o = _dot_nt(q_inter, state_t.astype(BF16)) + intra[bi, c, h]
                    kd_h = jnp.where(lane_head == sub, kd, 0.0).astype(BF16)
                    st_ref[bi, h] = state_t * decay + _dot_tn(vh_b, kd_h)
                    ms = jnp.mean(o * o, axis=-1, keepdims=True)
                    on = o * lax.rsqrt(ms + LN_EPS) * ng
                    rh = r_ref[bi, rows, vs]
                    o_ref[bi, rows, vs] = on * (rh * (1.0 / (1.0 + jnp.exp(-rh))))


def _gla(p, wg, bg, ng, batch, seq, ts, nb):
    kern = functools.partial(_gla_kernel, chunks=ts // CHUNK, nb=nb)
    return pl.pallas_call(
        kern,
        grid=(batch // nb, seq // ts),
        in_specs=[
            pl.BlockSpec((nb, ts, GLA_QK), lambda b, s: (b, s, COL_Q // GLA_QK)),
            pl.BlockSpec((nb, ts, GLA_QK), lambda b, s: (b, s, COL_K // GLA_QK)),
            pl.BlockSpec((nb, ts, GLA_WIDTH), lambda b, s: (b, s, COL_V // GLA_WIDTH)),
            pl.BlockSpec((nb, ts, GLA_WIDTH), lambda b, s: (b, s, COL_R // GLA_WIDTH)),
            pl.BlockSpec((nb, ts, A_PAD), lambda b, s: (b, s, COL_A // A_PAD)),
            pl.BlockSpec((A_PAD, GLA_QK), lambda b, s: (0, 0)),
            pl.BlockSpec((1, GLA_QK), lambda b, s: (0, 0)),
            pl.BlockSpec((1, GLA_DV), lambda b, s: (0, 0)),
        ],
        out_specs=pl.BlockSpec((nb, ts, GLA_WIDTH), lambda b, s: (b, s, 0)),
        out_shape=jax.ShapeDtypeStruct((batch, seq, GLA_WIDTH), F32),
        scratch_shapes=[pltpu.VMEM((nb, GLA_HEADS, GLA_DV, LANES), F32)],
        compiler_params=_params(("parallel", "arbitrary")),
        name="gla",
    )(p, p, p, p, p, wg, bg, ng)


def _mix_out_kernel(su_ref, sv_ref, o_ref, h_ref, lg_ref, lb_ref, ws_ref, bs_ref, wo_ref,
                    g1_ref, b1_ref, out_ref, gate_ref, *, windows):
    u = _gelu(su_ref[...])
    v = _layer_norm(_gelu(sv_ref[...]), lg_ref[...], lb_ref[...])
    blk_r = lax.broadcasted_iota(jnp.int32, (SGU_WINDOW, SGU_WINDOW), 0) // CHUNK
    blk_c = lax.broadcasted_iota(jnp.int32, (SGU_WINDOW, SGU_WINDOW), 1) // CHUNK
    keep = blk_c <= blk_r
    for g in range(SGU_GROUPS):
        cs = slice(g * SGU_GROUP_DIM, (g + 1) * SGU_GROUP_DIM)
        w = jnp.where(keep, ws_ref[g], 0.0).astype(BF16)
        bias = bs_ref[g]
        for n in range(windows):
            rs = slice(n * SGU_WINDOW, (n + 1) * SGU_WINDOW)
            sv = _dot(w, v[rs, cs].astype(BF16)) + bias
            gate_ref[rs, cs] = u[rs, cs] * sv
    mix = _dot(o_ref[...].astype(BF16), wo_ref[0:GLA_WIDTH, :])
    mix = mix + _dot(gate_ref[...].astype(BF16), wo_ref[GLA_WIDTH:, :])
    out_ref[...] = _layer_norm(ALPHA * h_ref[...] + mix, g1_ref[...], b1_ref[...])


def _mix_out(p, o, h, lg, lb, ws, bs_full, wo, g1, b1, tm):
    t, d = h.shape
    kern = functools.partial(_mix_out_kernel, windows=tm // SGU_WINDOW)
    const2 = lambda i: (0, 0)
    const3 = lambda i: (0, 0, 0)
    return pl.pallas_call(
        kern,
        grid=(t // tm,),
        in_specs=[
            pl.BlockSpec((tm, SGU_WIDTH), lambda i: (i, COL_SU // SGU_WIDTH)),
            pl.BlockSpec((tm, SGU_WIDTH), lambda i: (i, COL_SV // SGU_WIDTH)),
            pl.BlockSpec((tm, GLA_WIDTH), lambda i: (i, 0)),
            pl.BlockSpec((tm, d), lambda i: (i, 0)),
            pl.BlockSpec((1, SGU_WIDTH), const2),
            pl.BlockSpec((1, SGU_WIDTH), const2),
            pl.BlockSpec((SGU_GROUPS, SGU_WINDOW, SGU_WINDOW), const3),
            pl.BlockSpec((SGU_GROUPS, SGU_WINDOW, SGU_GROUP_DIM), const3),
            pl.BlockSpec((d, d), const2),
            pl.BlockSpec((1, d), const2),
            pl.BlockSpec((1, d), const2),
        ],
        out_specs=pl.BlockSpec((tm, d), lambda i: (i, 0)),
        out_shape=jax.ShapeDtypeStruct((t, d), F32),
        scratch_shapes=[pltpu.VMEM((tm, SGU_WIDTH), F32)],
        compiler_params=_params(("parallel",)),
        name="mix_out",
    )(p, p, o, h, lg, lb, ws, bs_full, wo, g1, b1)


def _sort_network(n):
    pairs = []

    def merge(lo, hi, r):
        step = r * 2
        if step < hi - lo:
            merge(lo, hi, step)
            merge(lo + r, hi, step)
            pairs.extend((i, i + r) for i in range(lo + r, hi - r, step))
        else:
            pairs.append((lo, lo + r))

    def sort(lo, hi):
        if hi - lo >= 1:
            mid = lo + (hi - lo) // 2
            sort(lo, mid)
            sort(mid + 1, hi)
            merge(lo, hi, 1)

    sort(0, n - 1)
    return pairs


def _ce(x, y):
    if y is None:
        return x, None
    if x is None:
        return y, None
    return jnp.maximum(x, y), jnp.minimum(x, y)


def _merge_top(a, b):
    n = len(a)
    v = [_ce(a[k], b[n - 1 - k])[0] for k in range(n)]
    d = n // 2
    while d >= 1:
        for i in range(n):
            if i & d == 0:
                v[i], v[i + d] = _ce(v[i], v[i + d])
        d //= 2
    return v


def _top_sorted(tiles):
    v = list(tiles)
    for i, j in _sort_network(len(v)):
        v[i], v[j] = _ce(v[i], v[j])
    shift = SUBLANES // 2
    while shift >= 1:
        v = _merge_top(v, [pltpu.roll(x, shift, axis=0) for x in v])
        shift //= 2
    return v


def _pad(vals):
    return list(vals) + [None] * (PEER_TOPK - len(vals))


def _fold_kernel(k_ref, wq_ref, o_ref):
    o_ref[...] = lax.dot_general(k_ref[...], wq_ref[...], (((1,), (1,)), ((), ())),
                                 preferred_element_type=F32,
                                 precision=lax.Precision.HIGHEST).astype(BF16)


def _fold_keys(k1, k2, wq):
    d = wq.shape[0]
    keys = jnp.stack([k1, k2])
    return pl.pallas_call(
        _fold_kernel,
        grid=(PEER_HEADS, 2),
        in_specs=[pl.BlockSpec((None, PEER_NKEYS, PEER_DQ_HALF), lambda h, s: (s, 0, 0)),
                  pl.BlockSpec((d, PEER_DQ_HALF), lambda h, s: (0, 2 * h + s))],
        out_specs=pl.BlockSpec((PEER_NKEYS, d), lambda h, s: (2 * h + s, 0)),
        out_shape=jax.ShapeDtypeStruct((PEER_HEADS * 2 * PEER_NKEYS, d), BF16),
        compiler_params=_params(("parallel", "parallel")),
        name="peer_fold",
    )(keys, wq)


def _route_kernel(h_ref, wall_ref, xt_ref, rank2_ref, p2_ref, n1_ref, p1_ref,
                  s_ref, b_ref, a0_ref, pa_ref, pb_ref, cut_ref, extra_ref, invz_ref):
    tm = h_ref.shape[0]
    xt = jnp.transpose(h_ref[...]).astype(BF16)
    xt_ref[...] = xt
    s_ref[...] = _dot(wall_ref[...], xt)
    ntile = PEER_NKEYS // SUBLANES

    for c in range(tm // LANES):
        ls = slice(c * LANES, (c + 1) * LANES)

        def tile(hd, half, k):
            r0 = pl.multiple_of(hd * (2 * PEER_NKEYS) + half * PEER_NKEYS + k * SUBLANES, SUBLANES)
            return s_ref[pl.ds(r0, SUBLANES), ls]

        def sort_body(hd, carry):
            a = _top_sorted([tile(hd, 0, k) for k in range(ntile)])
            b = _top_sorted([tile(hd, 1, k) for k in range(ntile)])
            a0_ref[hd] = a[0]
            for r in range(PEER_TOPK):
                b_ref[hd, r] = b[r]
                pa_ref[r, pl.ds(hd, 1), :] = a[r][0:1, :]
                pb_ref[r, pl.ds(hd, 1), :] = b[r][0:1, :]
            return carry

        lax.fori_loop(0, PEER_HEADS, sort_body, 0)

        a = [pa_ref[r] for r in range(PEER_TOPK)]
        b = [pb_ref[r] for r in range(PEER_TOPK)]
        row = [_pad([a[i] + b[j] for j in range(PEER_TOPK // (i + 1))]) for i in range(SUBLANES)]
        tail = _pad([a[i] + b[0] for i in range(SUBLANES, PEER_TOPK)])
        small = _merge_top(_merge_top(row[4], row[5]), _merge_top(row[6], row[7]))
        mid = _merge_top(_merge_top(row[2], row[3]), small)
        top = _merge_top(row[0], _merge_top(_merge_top(row[1], tail), mid))
        thr = top[PEER_TOPK - 1]
        z = None
        for r in range(PEER_TOPK):
            e = jnp.exp(top[r] - top[0])
            z = e if z is None else z + e
        invz_ref[...] = 1.0 / z
        for j in range(SUBLANES):
            cj = jnp.full_like(thr, float("inf"))
            for i in range(min(PEER_TOPK // (j + 1), SUBLANES)):
                cj = jnp.where(row[i][j] >= thr, a[i], cj)
            if j == 0:
                for i in range(SUBLANES, PEER_TOPK):
                    cj = jnp.where(tail[i - SUBLANES] >= thr, a[i], cj)
            cut_ref[j] = cj
        extra = jnp.zeros_like(thr)
        for j in range(SUBLANES, PEER_TOPK):
            extra = extra + jnp.where(row[0][j] >= thr, 1.0, 0.0)
        extra_ref[...] = extra

        def emit_body(hd, carry):
            head_row = lambda ref, *idx: jnp.broadcast_to(ref[(*idx, pl.ds(hd, 1), slice(None))],
                                                          (SUBLANES, LANES))
            cut = [head_row(cut_ref, j) for j in range(SUBLANES)]
            extra = head_row(extra_ref)
            inv_z = head_row(invz_ref)
            a0 = a0_ref[hd]
            bs = [b_ref[hd, r] for r in range(PEER_TOPK)]
            for kp in range(ntile // 2):
                r2, pp2 = [], []
                for k in (2 * kp, 2 * kp + 1):
                    rows = slice(k * SUBLANES, (k + 1) * SUBLANES)
                    s1k = tile(hd, 0, k)
                    n1 = jnp.zeros_like(s1k)
                    for j in range(SUBLANES):
                        n1 = jnp.where(s1k >= cut[j], float(j + 1), n1)
                    n1_ref[hd, rows, ls] = jnp.where(s1k >= a0, n1 + extra, n1)
                    p1_ref[hd, rows, ls] = jnp.exp(s1k - a0) * inv_z
                    s2k = tile(hd, 1, k)
                    rk = jnp.full_like(s2k, float(PEER_TOPK))
                    for r in range(PEER_TOPK - 1, -1, -1):
                        rk = jnp.where(s2k >= bs[r], float(r), rk)
                    r2.append(rk)
                    pp2.append(jnp.exp(s2k - bs[0]))
                rows2 = slice(kp * 2 * SUBLANES, (kp + 1) * 2 * SUBLANES)
                rank2_ref[hd, rows2, ls] = jnp.concatenate(r2, axis=0).astype(BF16)
                p2_ref[hd, rows2, ls] = jnp.concatenate(pp2, axis=0).astype(BF16)
            return carry

        lax.fori_loop(0, PEER_HEADS, emit_body, 0)


def _route(h, wall, tm):
    t, d = h.shape
    hk = (PEER_HEADS, PEER_NKEYS, t)
    blk = pl.BlockSpec((PEER_HEADS, PEER_NKEYS, tm), lambda i: (0, 0, i))
    vreg = (SUBLANES, LANES)
    return pl.pallas_call(
        _route_kernel,
        grid=(t // tm,),
        in_specs=[
            pl.BlockSpec((tm, d), lambda i: (i, 0)),
            pl.BlockSpec(wall.shape, lambda i: (0, 0)),
        ],
        out_specs=[pl.BlockSpec((d, tm), lambda i: (0, i)), blk, blk, blk, blk],
        out_shape=[
            jax.ShapeDtypeStruct((d, t), BF16),
            jax.ShapeDtypeStruct(hk, BF16),
            jax.ShapeDtypeStruct(hk, BF16),
            jax.ShapeDtypeStruct(hk, F32),
            jax.ShapeDtypeStruct(hk, F32),
        ],
        scratch_shapes=[pltpu.VMEM((wall.shape[0], tm), F32),
                        pltpu.VMEM((PEER_HEADS, PEER_TOPK) + vreg, F32),
                        pltpu.VMEM((PEER_HEADS,) + vreg, F32),
                        pltpu.VMEM((PEER_TOPK,) + vreg, F32),
                        pltpu.VMEM((PEER_TOPK,) + vreg, F32),
                        pltpu.VMEM((SUBLANES,) + vreg, F32),
                        pltpu.VMEM(vreg, F32),
                        pltpu.VMEM(vreg, F32)],
        compiler_params=_params(("parallel",)),
        name="peer_route",
    )(h, wall)


def _peer_kernel(xt_ref, u_ref, vt_ref, rank2_ref, p2_ref, n1_ref, p1_ref, h_ref, g_ref, b_ref,
                 out_ref, acc_ref, y_ref, *, rows):
    e = pl.program_id(1)

    @pl.when(e == 0)
    def _():
        acc_ref[...] = jnp.zeros_like(acc_ref)

    xt = xt_ref[...]
    tm = xt.shape[1]
    pack = 2 * SUBLANES
    slabs = PEER_NKEYS // pack
    zero = jnp.zeros((), BF16)

    def row_tile(ref, hd, l):
        r8 = jnp.broadcast_to(ref[hd, l:l + 1, :], (SUBLANES, tm))
        return jnp.concatenate([r8, r8], axis=0).astype(BF16)[None]

    for l in range(rows):
        es = slice(l * PEER_NKEYS, (l + 1) * PEER_NKEYS)
        act = _gelu(_dot(u_ref[es, :], xt).astype(BF16))
        w = None
        for hd in range(PEER_HEADS):
            keep = rank2_ref[hd].reshape(slabs, pack, tm) < row_tile(n1_ref, hd, l)
            term = jnp.where(keep, p2_ref[hd].reshape(slabs, pack, tm), zero) * row_tile(p1_ref, hd, l)
            w = term if w is None else w + term
        y_ref[es, :] = act * w.reshape(PEER_NKEYS, tm)
    acc_ref[...] += _dot(vt_ref[...], y_ref[...])

    @pl.when(e == pl.num_programs(1) - 1)
    def _():
        ffn = jnp.transpose(acc_ref[...])
        out_ref[...] = _layer_norm(ALPHA * h_ref[...] + ffn, g_ref[...], b_ref[...])


def _peer(xt, u, vt, rank2, p2, n1, p1, h, g, b, tm, te):
    t, d = h.shape
    rows = te // PEER_NKEYS
    kern = functools.partial(_peer_kernel, rows=rows)
    route_blk = pl.BlockSpec((PEER_HEADS, PEER_NKEYS, tm), lambda i, e: (0, 0, i))
    row_blk = pl.BlockSpec((PEER_HEADS, rows, tm), lambda i, e: (0, e, i))
    return pl.pallas_call(
        kern,
        grid=(t // tm, PEER_EXPERTS // te),
        in_specs=[
            pl.BlockSpec((d, tm), lambda i, e: (0, i)),
            pl.BlockSpec((te, d), lambda i, e: (e, 0)),
            pl.BlockSpec((d, te), lambda i, e: (0, e)),
            route_blk, route_blk, row_blk, row_blk,
            pl.BlockSpec((tm, d), lambda i, e: (i, 0), pipeline_mode=pl.Buffered(1)),
            pl.BlockSpec((1, d), lambda i, e: (0, 0)),
            pl.BlockSpec((1, d), lambda i, e: (0, 0)),
        ],
        out_specs=pl.BlockSpec((tm, d), lambda i, e: (i, 0)),
        out_shape=jax.ShapeDtypeStruct((t, d), F32),
        scratch_shapes=[pltpu.VMEM((d, tm), F32), pltpu.VMEM((te, tm), BF16)],
        compiler_params=_params(("parallel", "arbitrary")),
        name="peer_dense",
    )(xt, u, vt, rank2, p2, n1, p1, h, g, b)


def _pack_w_in(w):
    q, k, v, r, a, su, sv = jnp.split(
        w, [256, 512, 1024, 1536, 1552, 2064], axis=-1)
    a = jnp.pad(a, ((0, 0), (0, A_PAD - GLA_GATE_RANK)))
    return jnp.concatenate([q, k, v, r, su, sv, a], axis=-1).astype(BF16)


def _forward(x, ln_in_g, ln_in_b, w_in, w_gate_up, b_gate, gla_norm_g, sgu_ln_g, sgu_ln_b,
             sgu_w, sgu_b, w_out, ln1_g, ln1_b, peer_wq, peer_k1, peer_k2, peer_u, peer_v,
             ln2_g, ln2_b, *, tm_proj, ts_gla, nb_gla, tm_mix, tm_route, tm_peer, te_peer):
    batch, seq, d = x.shape
    t = batch * seq
    row = lambda p: p.reshape(1, -1)
    h = None
    for l in range(DEPTH):
        if l == 0:
            h, p = _in_proj_ln(x.reshape(t, d), ln_in_g, ln_in_b, _pack_w_in(w_in[l]), tm_proj)
        else:
            p = _in_proj(h, _pack_w_in(w_in[l]), tm_proj)
        wg = jnp.pad(w_gate_up[l], ((0, A_PAD - GLA_GATE_RANK), (0, 0)))
        o = _gla(p.reshape(batch, seq, P_COLS), wg, row(b_gate[l]), row(gla_norm_g[l]),
                 batch, seq, ts_gla, nb_gla)
        bs_full = jnp.broadcast_to(sgu_b[l][:, :, None],
                                   (SGU_GROUPS, SGU_WINDOW, SGU_GROUP_DIM))
        h = _mix_out(p, o.reshape(t, GLA_WIDTH), h, row(sgu_ln_g[l]), row(sgu_ln_b[l]),
                     sgu_w[l], bs_full, w_out[l].astype(BF16), row(ln1_g[l]), row(ln1_b[l]),
                     tm_mix)
        wall = _fold_keys(peer_k1[l], peer_k2[l], peer_wq[l])
        xt, rank2, p2, n1, p1 = _route(h, wall, tm_route)
        h = _peer(xt, peer_u[l].astype(BF16), jnp.transpose(peer_v[l].astype(BF16)),
                  rank2, p2, n1, p1, h, row(ln2_g[l]), row(ln2_b[l]), tm_peer, te_peer)
    return h.reshape(batch, seq, d)


def kernel(x, ln_in_g, ln_in_b, w_in, w_gate_up, b_gate, gla_norm_g, sgu_ln_g, sgu_ln_b, sgu_w,
           sgu_b, w_out, ln1_g, ln1_b, peer_wq, peer_k1, peer_k2, peer_u, peer_v, ln2_g, ln2_b):
    return _forward(x, ln_in_g, ln_in_b, w_in, w_gate_up, b_gate, gla_norm_g, sgu_ln_g,
                    sgu_ln_b, sgu_w, sgu_b, w_out, ln1_g, ln1_b, peer_wq, peer_k1, peer_k2,
                    peer_u, peer_v, ln2_g, ln2_b,
                    tm_proj=512, ts_gla=256, nb_gla=8, tm_mix=512, tm_route=512, tm_peer=512,
                    te_peer=4096)
```

```python
import functools
import math

import jax
import jax.numpy as jnp
from jax import lax
from jax.experimental import pallas as pl
from jax.experimental.pallas import tpu as pltpu

F32 = jnp.float32
BF16 = jnp.bfloat16

D_MODEL = 1024
DEPTH = 2
CHUNK = 64
GLA_HEADS = 4
GLA_DK = 64
GLA_DV = 128
GLA_WIDTH = GLA_HEADS * GLA_DV
GLA_QK = GLA_HEADS * GLA_DK
GLA_GATE_RANK = 16
GLA_GATE_NORMALIZER = 16.0
SGU_WIDTH = 512
SGU_GROUPS = 4
SGU_GROUP_DIM = 128
SGU_WINDOW = 128
PEER_HEADS = 8
PEER_NKEYS = 128
PEER_EXPERTS = PEER_NKEYS * PEER_NKEYS
PEER_TOPK = 16
PEER_DQ = 256
PEER_DQ_HALF = 128
LN_EPS = 1e-5
ALPHA = (2.0 * DEPTH) ** 0.25

LANES = 128
SUBLANES = 8

COL_Q = 0
COL_K = COL_Q + GLA_QK
COL_V = COL_K + GLA_QK
COL_R = COL_V + GLA_WIDTH
COL_SU = COL_R + GLA_WIDTH
COL_SV = COL_SU + SGU_WIDTH
COL_A = COL_SV + SGU_WIDTH
A_PAD = LANES
P_COLS = COL_A + A_PAD

SUB = 16
EXP_CLAMP = 80.0

V7X_VMEM_BYTES = 64 * 1024 * 1024
VMEM_LIMIT = V7X_VMEM_BYTES * 15 // 16


def _params(sem):
    return pltpu.CompilerParams(dimension_semantics=sem, vmem_limit_bytes=VMEM_LIMIT)


def _layer_norm(x, g, b):
    mu = jnp.mean(x, axis=-1, keepdims=True)
    xc = x - mu
    var = jnp.mean(xc * xc, axis=-1, keepdims=True)
    return xc * lax.rsqrt(var + LN_EPS) * g + b


def _gelu(x):
    return 0.5 * x * (1.0 + lax.erf(x * (1.0 / math.sqrt(2.0))))


def _dot(a, b):
    return jnp.dot(a, b, preferred_element_type=F32)


def _dot_nt(a, b):
    return lax.dot_general(a, b, (((1,), (1,)), ((), ())), preferred_element_type=F32)


def _dot_tn(a, b):
    return lax.dot_general(a, b, (((0,), (0,)), ((), ())), preferred_element_type=F32)


def _in_proj_ln_kernel(x_ref, g_ref, b_ref, w_ref, h_ref, o_ref):
    h = _layer_norm(x_ref[...], g_ref[...], b_ref[...])
    h_ref[...] = h
    o_ref[...] = _dot(h.astype(BF16), w_ref[...])


def _in_proj_ln(x, g, b, w, tm):
    t, d = x.shape
    n = w.shape[1]
    return pl.pallas_call(
        _in_proj_ln_kernel,
        grid=(t // tm,),
        in_specs=[pl.BlockSpec((tm, d), lambda i: (i, 0)),
                  pl.BlockSpec((1, d), lambda i: (0, 0)),
                  pl.BlockSpec((1, d), lambda i: (0, 0)),
                  pl.BlockSpec((d, n), lambda i: (0, 0))],
        out_specs=[pl.BlockSpec((tm, d), lambda i: (i, 0)),
                   pl.BlockSpec((tm, n), lambda i: (i, 0))],
        out_shape=[jax.ShapeDtypeStruct((t, d), F32), jax.ShapeDtypeStruct((t, n), F32)],
        compiler_params=_params(("parallel",)),
        name="in_proj_ln",
    )(x, g.reshape(1, d), b.reshape(1, d), w)


def _in_proj_kernel(h_ref, w_ref, o_ref):
    o_ref[...] = _dot(h_ref[...].astype(BF16), w_ref[...])


def _in_proj(h, w, tm):
    t, d = h.shape
    n = w.shape[1]
    return pl.pallas_call(
        _in_proj_kernel,
        grid=(t // tm,),
        in_specs=[pl.BlockSpec((tm, d), lambda i: (i, 0)),
                  pl.BlockSpec((d, n), lambda i: (0, 0))],
        out_specs=pl.BlockSpec((tm, n), lambda i: (i, 0)),
        out_shape=jax.ShapeDtypeStruct((t, n), F32),
        compiler_params=_params(("parallel",)),
        name="in_proj",
    )(h, w)


def _gla_kernel(q_ref, k_ref, v_ref, r_ref, a_ref, wg_ref, bg_ref, ng_ref, o_ref, st_ref,
                *, chunks, nb):
    @pl.when(pl.program_id(1) == 0)
    def _():
        st_ref[...] = jnp.zeros_like(st_ref)

    ts = chunks * CHUNK
    row = lax.broadcasted_iota(jnp.int32, (CHUNK, CHUNK), 0)
    col = lax.broadcasted_iota(jnp.int32, (CHUNK, CHUNK), 1)
    causal = col <= row
    trow = lax.broadcasted_iota(jnp.int32, (ts, ts), 0)
    tcol = lax.broadcasted_iota(jnp.int32, (ts, ts), 1)
    tril = jnp.logical_and(tcol <= trow, tcol // CHUNK == trow // CHUNK).astype(F32)
    lane_head = lax.broadcasted_iota(jnp.int32, (1, LANES), 1) // GLA_DK
    ng = ng_ref[...]
    groups = GLA_QK // LANES
    per_group = LANES // GLA_DK
    items = [(bi, c) for bi in range(nb) for c in range(chunks)]

    bc = []
    for bi in range(nb):
        z = jnp.dot(a_ref[bi], wg_ref[...], preferred_element_type=F32,
                    precision=lax.Precision.HIGHEST) + bg_ref[...]
        logg = (jnp.minimum(z, 0.0) - jnp.log1p(jnp.exp(-jnp.abs(z)))) * (1.0 / GLA_GATE_NORMALIZER)
        bc.append(jnp.dot(tril, logg, preferred_element_type=F32, precision=lax.Precision.HIGHEST))

    scores = {}
    for bi, c in items:
        rows = slice(c * CHUNK, (c + 1) * CHUNK)
        for grp in range(groups):
            ls = slice(grp * LANES, (grp + 1) * LANES)
            qg = q_ref[bi, rows, ls] * (GLA_DK ** -0.5)
            kg = k_ref[bi, rows, ls]
            bg = bc[bi][rows, ls]
            parts = []
            for i in range(CHUNK // SUB):
                lo, hi = i * SUB, (i + 1) * SUB
                ref = bg[lo:lo + 1, :]
                qi = qg[lo:hi, :] * jnp.exp(bg[lo:hi, :] - ref)
                ki = (kg * jnp.exp(jnp.minimum(ref - bg, EXP_CLAMP))).astype(BF16)
                q2 = jnp.concatenate([jnp.where(lane_head == 0, qi, 0.0),
                                      jnp.where(lane_head == 1, qi, 0.0)], axis=0).astype(BF16)
                parts.append(_dot_nt(q2, ki))
            scores[bi, c, grp] = parts

    intra = {}
    for bi, c in items:
        rows = slice(c * CHUNK, (c + 1) * CHUNK)
        for h in range(GLA_HEADS):
            grp, sub = divmod(h, per_group)
            att = jnp.concatenate([p[sub * SUB:(sub + 1) * SUB, :] for p in scores[bi, c, grp]],
                                  axis=0)
            a_h = jnp.where(causal, att, 0.0).astype(BF16)
            vh_b = v_ref[bi, rows, h * GLA_DV:(h + 1) * GLA_DV].astype(BF16)
            intra[bi, c, h] = _dot(a_h, vh_b)

    for c in range(chunks):
        rows = slice(c * CHUNK, (c + 1) * CHUNK)
        for bi in range(nb):
            for grp in range(groups):
                ls = slice(grp * LANES, (grp + 1) * LANES)
                qg = q_ref[bi, rows, ls] * (GLA_DK ** -0.5)
                kg = k_ref[bi, rows, ls]
                bg = bc[bi][rows, ls]
                b_last = bg[CHUNK - 1:CHUNK, :]
                q_inter = (qg * jnp.exp(bg)).astype(BF16)
                kd = kg * jnp.exp(b_last - bg)
                decay = jnp.exp(b_last)
                for sub in range(per_group):
                    h = grp * per_group + sub
                    vs = slice(h * GLA_DV, (h + 1) * GLA_DV)
                    vh_b = v_ref[bi, rows, vs].astype(BF16)
                    state_t = st_ref[bi, h]
                    o = _dot_nt(q_inter, state_t.astype(BF16)) + intra[bi, c, h]
                    kd_h = jnp.where(lane_head == sub, kd, 0.0).astype(BF16)
                    st_ref[bi, h] = state_t * decay + _dot_tn(vh_b, kd_h)
                    ms = jnp.mean(o * o, axis=-1, keepdims=True)
                    on = o * lax.rsqrt(ms + LN_EPS) * ng
                    rh = r_ref[bi, rows, vs]
                    o_ref[bi, rows, vs] = on * (rh * (1.0 / (1.0 + jnp.exp(-rh))))


def _gla(p, wg, bg, ng, batch, seq, ts, nb):
    kern = functools.partial(_gla_kernel, chunks=ts // CHUNK, nb=nb)
    return pl.pallas_call(
        kern,
        grid=(batch // nb, seq // ts),
        in_specs=[
            pl.BlockSpec((nb, ts, GLA_QK), lambda b, s: (b, s, COL_Q // GLA_QK)),
            pl.BlockSpec((nb, ts, GLA_QK), lambda b, s: (b, s, COL_K // GLA_QK)),
            pl.BlockSpec((nb, ts, GLA_WIDTH), lambda b, s: (b, s, COL_V // GLA_WIDTH)),
            pl.BlockSpec((nb, ts, GLA_WIDTH), lambda b, s: (b, s, COL_R // GLA_WIDTH)),
            pl.BlockSpec((nb, ts, A_PAD), lambda b, s: (b, s, COL_A // A_PAD)),
            pl.BlockSpec((A_PAD, GLA_QK), lambda b, s: (0, 0)),
            pl.BlockSpec((1, GLA_QK), lambda b, s: (0, 0)),
            pl.BlockSpec((1, GLA_DV), lambda b, s: (0, 0)),
        ],
        out_specs=pl.BlockSpec((nb, ts, GLA_WIDTH), lambda b, s: (b, s, 0)),
        out_shape=jax.ShapeDtypeStruct((batch, seq, GLA_WIDTH), F32),
        scratch_shapes=[pltpu.VMEM((nb, GLA_HEADS, GLA_DV, LANES), F32)],
        compiler_params=_params(("parallel", "arbitrary")),
        name="gla",
    )(p, p, p, p, p, wg, bg, ng)


def _mix_out_kernel(su_ref, sv_ref, o_ref, h_ref, lg_ref, lb_ref, ws_ref, bs_ref, wo_ref,
                    g1_ref, b1_ref, out_ref, gate_ref, *, windows):
    u = _gelu(su_ref[...])
    v = _layer_norm(_gelu(sv_ref[...]), lg_ref[...], lb_ref[...])
    blk_r = lax.broadcasted_iota(jnp.int32, (SGU_WINDOW, SGU_WINDOW), 0) // CHUNK
    blk_c = lax.broadcasted_iota(jnp.int32, (SGU_WINDOW, SGU_WINDOW), 1) // CHUNK
    keep = blk_c <= blk_r
    for g in range(SGU_GROUPS):
        cs = slice(g * SGU_GROUP_DIM, (g + 1) * SGU_GROUP_DIM)
        w = jnp.where(keep, ws_ref[g], 0.0).astype(BF16)
        bias = bs_ref[g]
        for n in range(windows):
            rs = slice(n * SGU_WINDOW, (n + 1) * SGU_WINDOW)
            sv = _dot(w, v[rs, cs].astype(BF16)) + bias
            gate_ref[rs, cs] = u[rs, cs] * sv
    mix = _dot(o_ref[...].astype(BF16), wo_ref[0:GLA_WIDTH, :])
    mix = mix + _dot(gate_ref[...].astype(BF16), wo_ref[GLA_WIDTH:, :])
    out_ref[...] = _layer_norm(ALPHA * h_ref[...] + mix, g1_ref[...], b1_ref[...])


def _mix_out(p, o, h, lg, lb, ws, bs_full, wo, g1, b1, tm):
    t, d = h.shape
    kern = functools.partial(_mix_out_kernel, windows=tm // SGU_WINDOW)
    const2 = lambda i: (0, 0)
    const3 = lambda i: (0, 0, 0)
    return pl.pallas_call(
        kern,
        grid=(t // tm,),
        in_specs=[
            pl.BlockSpec((tm, SGU_WIDTH), lambda i: (i, COL_SU // SGU_WIDTH)),
            pl.BlockSpec((tm, SGU_WIDTH), lambda i: (i, COL_SV // SGU_WIDTH)),
            pl.BlockSpec((tm, GLA_WIDTH), lambda i: (i, 0)),
            pl.BlockSpec((tm, d), lambda i: (i, 0)),
            pl.BlockSpec((1, SGU_WIDTH), const2),
            pl.BlockSpec((1, SGU_WIDTH), const2),
            pl.BlockSpec((SGU_GROUPS, SGU_WINDOW, SGU_WINDOW), const3),
            pl.BlockSpec((SGU_GROUPS, SGU_WINDOW, SGU_GROUP_DIM), const3),
            pl.BlockSpec((d, d), const2),
            pl.BlockSpec((1, d), const2),
            pl.BlockSpec((1, d), const2),
        ],
        out_specs=pl.BlockSpec((tm, d), lambda i: (i, 0)),
        out_shape=jax.ShapeDtypeStruct((t, d), F32),
        scratch_shapes=[pltpu.VMEM((tm, SGU_WIDTH), F32)],
        compiler_params=_params(("parallel",)),
        name="mix_out",
    )(p, p, o, h, lg, lb, ws, bs_full, wo, g1, b1)


def _sort_network(n):
    pairs = []

    def merge(lo, hi, r):
        step = r * 2
        if step < hi - lo:
            merge(lo, hi, step)
            merge(lo + r, hi, step)
            pairs.extend((i, i + r) for i in range(lo + r, hi - r, step))
        else:
            pairs.append((lo, lo + r))

    def sort(lo, hi):
        if hi - lo >= 1:
            mid = lo + (hi - lo) // 2
            sort(lo, mid)
            sort(mid + 1, hi)
            merge(lo, hi, 1)

    sort(0, n - 1)
    return pairs


def _ce(x, y):
    if y is None:
        return x, None
    if x is None:
        return y, None
    return jnp.maximum(x, y), jnp.minimum(x, y)


def _merge_top(a, b):
    n = len(a)
    v = [_ce(a[k], b[n - 1 - k])[0] for k in range(n)]
    d = n // 2
    while d >= 1:
        for i in range(n):
            if i & d == 0:
                v[i], v[i + d] = _ce(v[i], v[i + d])
        d //= 2
    return v


def _top_sorted(tiles):
    v = list(tiles)
    for i, j in _sort_network(len(v)):
        v[i], v[j] = _ce(v[i], v[j])
    shift = SUBLANES // 2
    while shift >= 1:
        v = _merge_top(v, [pltpu.roll(x, shift, axis=0) for x in v])
        shift //= 2
    return v


def _pad(vals):
    return list(vals) + [None] * (PEER_TOPK - len(vals))


def _fold_kernel(k_ref, wq_ref, o_ref):
    o_ref[...] = lax.dot_general(k_ref[...], wq_ref[...], (((1,), (1,)), ((), ())),
                                 preferred_element_type=F32,
                                 precision=lax.Precision.HIGHEST).astype(BF16)


def _fold_keys(k1, k2, wq):
    d = wq.shape[0]
    keys = jnp.stack([k1, k2])
    return pl.pallas_call(
        _fold_kernel,
        grid=(PEER_HEADS, 2),
        in_specs=[pl.BlockSpec((None, PEER_NKEYS, PEER_DQ_HALF), lambda h, s: (s, 0, 0)),
                  pl.BlockSpec((d, PEER_DQ_HALF), lambda h, s: (0, 2 * h + s))],
        out_specs=pl.BlockSpec((PEER_NKEYS, d), lambda h, s: (2 * h + s, 0)),
        out_shape=jax.ShapeDtypeStruct((PEER_HEADS * 2 * PEER_NKEYS, d), BF16),
        compiler_params=_params(("parallel", "parallel")),
        name="peer_fold",
    )(keys, wq)


def _route_kernel(h_ref, wall_ref, xt_ref, rank2_ref, p2_ref, n1_ref, p1_ref,
                  s_ref, b_ref, a0_ref, pa_ref, pb_ref, cut_ref, extra_ref, invz_ref):
    tm = h_ref.shape[0]
    xt = jnp.transpose(h_ref[...]).astype(BF16)
    xt_ref[...] = xt
    s_ref[...] = _dot(wall_ref[...], xt)
    ntile = PEER_NKEYS // SUBLANES

    for c in range(tm // LANES):
        ls = slice(c * LANES, (c + 1) * LANES)

        def tile(hd, half, k):
            r0 = pl.multiple_of(hd * (2 * PEER_NKEYS) + half * PEER_NKEYS + k * SUBLANES, SUBLANES)
            return s_ref[pl.ds(r0, SUBLANES), ls]

        def sort_body(hd, carry):
            a = _top_sorted([tile(hd, 0, k) for k in range(ntile)])
            b = _top_sorted([tile(hd, 1, k) for k in range(ntile)])
            a0_ref[hd] = a[0]
            for r in range(PEER_TOPK):
                b_ref[hd, r] = b[r]
                pa_ref[r, pl.ds(hd, 1), :] = a[r][0:1, :]
                pb_ref[r, pl.ds(hd, 1), :] = b[r][0:1, :]
            return carry

        lax.fori_loop(0, PEER_HEADS, sort_body, 0)

        a = [pa_ref[r] for r in range(PEER_TOPK)]
        b = [pb_ref[r] for r in range(PEER_TOPK)]
        row = [_pad([a[i] + b[j] for j in range(PEER_TOPK // (i + 1))]) for i in range(SUBLANES)]
        tail = _pad([a[i] + b[0] for i in range(SUBLANES, PEER_TOPK)])
        small = _merge_top(_merge_top(row[4], row[5]), _merge_top(row[6], row[7]))
        mid = _merge_top(_merge_top(row[2], row[3]), small)
        top = _merge_top(row[0], _merge_top(_merge_top(row[1], tail), mid))
        thr = top[PEER_TOPK - 1]
        z = None
        for r in range(PEER_TOPK):
            e = jnp.exp(top[r] - top[0])
            z = e if z is None else z + e
        invz_ref[...] = 1.0 / z
        for j in range(SUBLANES):
            cj = jnp.full_like(thr, float("inf"))
            for i in range(min(PEER_TOPK // (j + 1), SUBLANES)):
                cj = jnp.where(row[i][j] >= thr, a[i], cj)
            if j == 0:
                for i in range(SUBLANES, PEER_TOPK):
                    cj = jnp.where(tail[i - SUBLANES] >= thr, a[i], cj)
            cut_ref[j] = cj
        extra = jnp.zeros_like(thr)
        for j in range(SUBLANES, PEER_TOPK):
            extra = extra + jnp.where(row[0][j] >= thr, 1.0, 0.0)
        extra_ref[...] = extra

        def emit_body(hd, carry):
            head_row = lambda ref, *idx: jnp.broadcast_to(ref[(*idx, pl.ds(hd, 1), slice(None))],
                                                          (SUBLANES, LANES))
            cut = [head_row(cut_ref, j) for j in range(SUBLANES)]
            extra = head_row(extra_ref)
            inv_z = head_row(invz_ref)
            a0 = a0_ref[hd]
            bs = [b_ref[hd, r] for r in range(PEER_TOPK)]
            for kp in range(ntile // 2):
                r2, pp2 = [], []
                for k in (2 * kp, 2 * kp + 1):
                    rows = slice(k * SUBLANES, (k + 1) * SUBLANES)
                    s1k = tile(hd, 0, k)
                    n1 = jnp.zeros_like(s1k)
                    for j in range(SUBLANES):
                        n1 = jnp.where(s1k >= cut[j], float(j + 1), n1)
                    n1_ref[hd, rows, ls] = jnp.where(s1k >= a0, n1 + extra, n1)
                    p1_ref[hd, rows, ls] = jnp.exp(s1k - a0) * inv_z
                    s2k = tile(hd, 1, k)
                    rk = jnp.full_like(s2k, float(PEER_TOPK))
                    for r in range(PEER_TOPK - 1, -1, -1):
                        rk = jnp.where(s2k >= bs[r], float(r), rk)
                    r2.append(rk)
                    pp2.append(jnp.exp(s2k - bs[0]))
                rows2 = slice(kp * 2 * SUBLANES, (kp + 1) * 2 * SUBLANES)
                rank2_ref[hd, rows2, ls] = jnp.concatenate(r2, axis=0).astype(BF16)
                p2_ref[hd, rows2, ls] = jnp.concatenate(pp2, axis=0).astype(BF16)
            return carry

        lax.fori_loop(0, PEER_HEADS, emit_body, 0)


def _route(h, wall, tm):
    t, d = h.shape
    hk = (PEER_HEADS, PEER_NKEYS, t)
    blk = pl.BlockSpec((PEER_HEADS, PEER_NKEYS, tm), lambda i: (0, 0, i))
    vreg = (SUBLANES, LANES)
    return pl.pallas_call(
        _route_kernel,
        grid=(t // tm,),
        in_specs=[
            pl.BlockSpec((tm, d), lambda i: (i, 0)),
            pl.BlockSpec(wall.shape, lambda i: (0, 0)),
        ],
        out_specs=[pl.BlockSpec((d, tm), lambda i: (0, i)), blk, blk, blk, blk],
        out_shape=[
            jax.ShapeDtypeStruct((d, t), BF16),
            jax.ShapeDtypeStruct(hk, BF16),
            jax.ShapeDtypeStruct(hk, BF16),
            jax.ShapeDtypeStruct(hk, F32),
            jax.ShapeDtypeStruct(hk, F32),
        ],
        scratch_shapes=[pltpu.VMEM((wall.shape[0], tm), F32),
                        pltpu.VMEM((PEER_HEADS, PEER_TOPK) + vreg, F32),
                        pltpu.VMEM((PEER_HEADS,) + vreg, F32),
                        pltpu.VMEM((PEER_TOPK,) + vreg, F32),
                        pltpu.VMEM((PEER_TOPK,) + vreg, F32),
                        pltpu.VMEM((SUBLANES,) + vreg, F32),
                        pltpu.VMEM(vreg, F32),
                        pltpu.VMEM(vreg, F32)],
        compiler_params=_params(("parallel",)),
        name="peer_route",
    )(h, wall)


def _peer_kernel(xt_ref, u_ref, vt_ref, rank2_ref, p2_ref, n1_ref, p1_ref, h_ref, g_ref, b_ref,
                 out_ref, acc_ref, y_ref, *, rows):
    e = pl.program_id(1)

    @pl.when(e == 0)
    def _():
        acc_ref[...] = jnp.zeros_like(acc_ref)

    xt = xt_ref[...]
    tm = xt.shape[1]
    pack = 2 * SUBLANES
    slabs = PEER_NKEYS // pack
    zero = jnp.zeros((), BF16)

    def row_tile(ref, hd, l):
        r8 = jnp.broadcast_to(ref[hd, l:l + 1, :], (SUBLANES, tm))
        return jnp.concatenate([r8, r8], axis=0).astype(BF16)[None]

    for l in range(rows):
        es = slice(l * PEER_NKEYS, (l + 1) * PEER_NKEYS)
        act = _gelu(_dot(u_ref[es, :], xt).astype(BF16))
        w = None
        for hd in range(PEER_HEADS):
            keep = rank2_ref[hd].reshape(slabs, pack, tm) < row_tile(n1_ref, hd, l)
            term = jnp.where(keep, p2_ref[hd].reshape(slabs, pack, tm), zero) * row_tile(p1_ref, hd, l)
            w = term if w is None else w + term
        y_ref[es, :] = act * w.reshape(PEER_NKEYS, tm)
    acc_ref[...] += _dot(vt_ref[...], y_ref[...])

    @pl.when(e == pl.num_programs(1) - 1)
    def _():
        ffn = jnp.transpose(acc_ref[...])
        out_ref[...] = _layer_norm(ALPHA * h_ref[...] + ffn, g_ref[...], b_ref[...])


def _peer(xt, u, vt, rank2, p2, n1, p1, h, g, b, tm, te):
    t, d = h.shape
    rows = te // PEER_NKEYS
    kern = functools.partial(_peer_kernel, rows=rows)
    route_blk = pl.BlockSpec((PEER_HEADS, PEER_NKEYS, tm), lambda i, e: (0, 0, i))
    row_blk = pl.BlockSpec((PEER_HEADS, rows, tm), lambda i, e: (0, e, i))
    return pl.pallas_call(
        kern,
        grid=(t // tm, PEER_EXPERTS // te),
        in_specs=[
            pl.BlockSpec((d, tm), lambda i, e: (0, i)),
            pl.BlockSpec((te, d), lambda i, e: (e, 0)),
            pl.BlockSpec((d, te), lambda i, e: (0, e)),
            route_blk, route_blk, row_blk, row_blk,
            pl.BlockSpec((tm, d), lambda i, e: (i, 0), pipeline_mode=pl.Buffered(1)),
            pl.BlockSpec((1, d), lambda i, e: (0, 0)),
            pl.BlockSpec((1, d), lambda i, e: (0, 0)),
        ],
        out_specs=pl.BlockSpec((tm, d), lambda i, e: (i, 0)),
        out_shape=jax.ShapeDtypeStruct((t, d), F32),
        scratch_shapes=[pltpu.VMEM((d, tm), F32), pltpu.VMEM((te, tm), BF16)],
        compiler_params=_params(("parallel", "arbitrary")),
        name="peer_dense",
    )(xt, u, vt, rank2, p2, n1, p1, h, g, b)


def _pack_w_in(w):
    q, k, v, r, a, su, sv = jnp.split(
        w, [256, 512, 1024, 1536, 1552, 2064], axis=-1)
    a = jnp.pad(a, ((0, 0), (0, A_PAD - GLA_GATE_RANK)))
    return jnp.concatenate([q, k, v, r, su, sv, a], axis=-1).astype(BF16)


def _forward(x, ln_in_g, ln_in_b, w_in, w_gate_up, b_gate, gla_norm_g, sgu_ln_g, sgu_ln_b,
             sgu_w, sgu_b, w_out, ln1_g, ln1_b, peer_wq, peer_k1, peer_k2, peer_u, peer_v,
             ln2_g, ln2_b, *, tm_proj, ts_gla, nb_gla, tm_mix, tm_route, tm_peer, te_peer):
    batch, seq, d = x.shape
    t = batch * seq
    row = lambda p: p.reshape(1, -1)
    h = None
    for l in range(DEPTH):
        if l == 0:
            h, p = _in_proj_ln(x.reshape(t, d), ln_in_g, ln_in_b, _pack_w_in(w_in[l]), tm_proj)
        else:
            p = _in_proj(h, _pack_w_in(w_in[l]), tm_proj)
        wg = jnp.pad(w_gate_up[l], ((0, A_PAD - GLA_GATE_RANK), (0, 0)))
        o = _gla(p.reshape(batch, seq, P_COLS), wg, row(b_gate[l]), row(gla_norm_g[l]),
                 batch, seq, ts_gla, nb_gla)
        bs_full = jnp.broadcast_to(sgu_b[l][:, :, None],
                                   (SGU_GROUPS, SGU_WINDOW, SGU_GROUP_DIM))
        h = _mix_out(p, o.reshape(t, GLA_WIDTH), h, row(sgu_ln_g[l]), row(sgu_ln_b[l]),
                     sgu_w[l], bs_full, w_out[l].astype(BF16), row(ln1_g[l]), row(ln1_b[l]),
                     tm_mix)
        wall = _fold_keys(peer_k1[l], peer_k2[l], peer_wq[l])
        xt, rank2, p2, n1, p1 = _route(h, wall, tm_route)
        h = _peer(xt, peer_u[l].astype(BF16), jnp.transpose(peer_v[l].astype(BF16)),
                  rank2, p2, n1, p1, h, row(ln2_g[l]), row(ln2_b[l]), tm_peer, te_peer)
    return h.reshape(batch, seq, d)


def kernel(x, ln_in_g, ln_in_b, w_in, w_gate_up, b_gate, gla_norm_g, sgu_ln_g, sgu_ln_b, sgu_w,
           sgu_b, w_out, ln1_g, ln1_b, peer_wq, peer_k1, peer_k2, peer_u, peer_v, ln2_g, ln2_b):
    return _forward(x, ln_in_g, ln_in_b, w_in, w_gate_up, b_gate, gla_norm_g, sgu_ln_g,
                    sgu_ln_b, sgu_w, sgu_b, w_out, ln1_g, ln1_b, peer_wq, peer_k1, peer_k2,
                    peer_u, peer_v, ln2_g, ln2_b,
                    tm_proj=512, ts_gla=256, nb_gla=8, tm_mix=512, tm_route=512, tm_peer=1024,
                    te_peer=2048)
```

```python
import functools
import math

import jax
import jax.numpy as jnp
from jax import lax
from jax.experimental import pallas as pl
from jax.experimental.pallas import tpu as pltpu

F32 = jnp.float32
BF16 = jnp.bfloat16

D_MODEL = 1024
DEPTH = 2
CHUNK = 64
GLA_HEADS = 4
GLA_DK = 64
GLA_DV = 128
GLA_WIDTH = GLA_HEADS * GLA_DV
GLA_QK = GLA_HEADS * GLA_DK
GLA_GATE_RANK = 16
GLA_GATE_NORMALIZER = 16.0
SGU_WIDTH = 512
SGU_GROUPS = 4
SGU_GROUP_DIM = 128
SGU_WINDOW = 128
PEER_HEADS = 8
PEER_NKEYS = 128
PEER_EXPERTS = PEER_NKEYS * PEER_NKEYS
PEER_TOPK = 16
PEER_DQ = 256
PEER_DQ_HALF = 128
LN_EPS = 1e-5
ALPHA = (2.0 * DEPTH) ** 0.25

LANES = 128
SUBLANES = 8

COL_Q = 0
COL_K = COL_Q + GLA_QK
COL_V = COL_K + GLA_QK
COL_R = COL_V + GLA_WIDTH
COL_SU = COL_R + GLA_WIDTH
COL_SV = COL_SU + SGU_WIDTH
COL_A = COL_SV + SGU_WIDTH
A_PAD = LANES
P_COLS = COL_A + A_PAD

SUB = 16
PEER_RING = 3
EXP_CLAMP = 80.0

V7X_VMEM_BYTES = 64 * 1024 * 1024
VMEM_LIMIT = V7X_VMEM_BYTES * 7 // 8


def _params(sem):
    return pltpu.CompilerParams(dimension_semantics=sem, vmem_limit_bytes=VMEM_LIMIT)


def _layer_norm(x, g, b):
    mu = jnp.mean(x, axis=-1, keepdims=True)
    xc = x - mu
    var = jnp.mean(xc * xc, axis=-1, keepdims=True)
    return xc * lax.rsqrt(var + LN_EPS) * g + b


def _gelu(x):
    return 0.5 * x * (1.0 + lax.erf(x * (1.0 / math.sqrt(2.0))))


def _dot(a, b):
    return jnp.dot(a, b, preferred_element_type=F32)


def _dot_nt(a, b):
    return lax.dot_general(a, b, (((1,), (1,)), ((), ())), preferred_element_type=F32)


def _dot_tn(a, b):
    return lax.dot_general(a, b, (((0,), (0,)), ((), ())), preferred_element_type=F32)


def _in_proj_ln_kernel(x_ref, g_ref, b_ref, w_ref, h_ref, o_ref):
    h = _layer_norm(x_ref[...], g_ref[...], b_ref[...])
    h_ref[...] = h
    o_ref[...] = _dot(h.astype(BF16), w_ref[...])


def _in_proj_ln(x, g, b, w, tm):
    t, d = x.shape
    n = w.shape[1]
    return pl.pallas_call(
        _in_proj_ln_kernel,
        grid=(t // tm,),
        in_specs=[pl.BlockSpec((tm, d), lambda i: (i, 0)),
                  pl.BlockSpec((1, d), lambda i: (0, 0)),
                  pl.BlockSpec((1, d), lambda i: (0, 0)),
                  pl.BlockSpec((d, n), lambda i: (0, 0))],
        out_specs=[pl.BlockSpec((tm, d), lambda i: (i, 0)),
                   pl.BlockSpec((tm, n), lambda i: (i, 0))],
        out_shape=[jax.ShapeDtypeStruct((t, d), F32), jax.ShapeDtypeStruct((t, n), F32)],
        compiler_params=_params(("parallel",)),
        name="in_proj_ln",
    )(x, g.reshape(1, d), b.reshape(1, d), w)


def _in_proj_kernel(h_ref, w_ref, o_ref):
    o_ref[...] = _dot(h_ref[...].astype(BF16), w_ref[...])


def _in_proj(h, w, tm):
    t, d = h.shape
    n = w.shape[1]
    return pl.pallas_call(
        _in_proj_kernel,
        grid=(t // tm,),
        in_specs=[pl.BlockSpec((tm, d), lambda i: (i, 0)),
                  pl.BlockSpec((d, n), lambda i: (0, 0))],
        out_specs=pl.BlockSpec((tm, n), lambda i: (i, 0)),
        out_shape=jax.ShapeDtypeStruct((t, n), F32),
        compiler_params=_params(("parallel",)),
        name="in_proj",
    )(h, w)


def _gla_kernel(q_ref, k_ref, v_ref, r_ref, a_ref, wg_ref, bg_ref, ng_ref, o_ref, st_ref,
                *, chunks, nb):
    @pl.when(pl.program_id(1) == 0)
    def _():
        st_ref[...] = jnp.zeros_like(st_ref)

    ts = chunks * CHUNK
    row = lax.broadcasted_iota(jnp.int32, (CHUNK, CHUNK), 0)
    col = lax.broadcasted_iota(jnp.int32, (CHUNK, CHUNK), 1)
    causal = col <= row
    trow = lax.broadcasted_iota(jnp.int32, (ts, ts), 0)
    tcol = lax.broadcasted_iota(jnp.int32, (ts, ts), 1)
    tril = jnp.logical_and(tcol <= trow, tcol // CHUNK == trow // CHUNK).astype(F32)
    lane_head = lax.broadcasted_iota(jnp.int32, (1, LANES), 1) // GLA_DK
    ng = ng_ref[...]
    groups = GLA_QK // LANES
    per_group = LANES // GLA_DK
    items = [(bi, c) for bi in range(nb) for c in range(chunks)]

    bc = []
    for bi in range(nb):
        z = jnp.dot(a_ref[bi], wg_ref[...], preferred_element_type=F32,
                    precision=lax.Precision.HIGHEST) + bg_ref[...]
        logg = (jnp.minimum(z, 0.0) - jnp.log1p(jnp.exp(-jnp.abs(z)))) * (1.0 / GLA_GATE_NORMALIZER)
        bc.append(jnp.dot(tril, logg, preferred_element_type=F32, precision=lax.Precision.HIGHEST))

    scores = {}
    for bi, c in items:
        rows = slice(c * CHUNK, (c + 1) * CHUNK)
        for grp in range(groups):
            ls = slice(grp * LANES, (grp + 1) * LANES)
            qg = q_ref[bi, rows, ls] * (GLA_DK ** -0.5)
            kg = k_ref[bi, rows, ls]
            bg = bc[bi][rows, ls]
            parts = []
            for i in range(CHUNK // SUB):
                lo, hi = i * SUB, (i + 1) * SUB
                ref = bg[lo:lo + 1, :]
                qi = qg[lo:hi, :] * jnp.exp(bg[lo:hi, :] - ref)
                ki = (kg * jnp.exp(jnp.minimum(ref - bg, EXP_CLAMP))).astype(BF16)
                q2 = jnp.concatenate([jnp.where(lane_head == 0, qi, 0.0),
                                      jnp.where(lane_head == 1, qi, 0.0)], axis=0).astype(BF16)
                parts.append(_dot_nt(q2, ki))
            scores[bi, c, grp] = parts

    intra = {}
    for bi, c in items:
        rows = slice(c * CHUNK, (c + 1) * CHUNK)
        for h in range(GLA_HEADS):
            grp, sub = divmod(h, per_group)
            att = jnp.concatenate([p[sub * SUB:(sub + 1) * SUB, :] for p in scores[bi, c, grp]],
                                  axis=0)
            a_h = jnp.where(causal, att, 0.0).astype(BF16)
            vh_b = v_ref[bi, rows, h * GLA_DV:(h + 1) * GLA_DV].astype(BF16)
            intra[bi, c, h] = _dot(a_h, vh_b)

    for c in range(chunks):
        rows = slice(c * CHUNK, (c + 1) * CHUNK)
        for bi in range(nb):
            for grp in range(groups):
                ls = slice(grp * LANES, (grp + 1) * LANES)
                qg = q_ref[bi, rows, ls] * (GLA_DK ** -0.5)
                kg = k_ref[bi, rows, ls]
                bg = bc[bi][rows, ls]
                b_last = bg[CHUNK - 1:CHUNK, :]
                q_inter = (qg * jnp.exp(bg)).astype(BF16)
                kd = kg * jnp.exp(b_last - bg)
                decay = jnp.exp(b_last)
                for sub in range(per_group):
                    h = grp * per_group + sub
                    vs = slice(h * GLA_DV, (h + 1) * GLA_DV)
                    vh_b = v_ref[bi, rows, vs].astype(BF16)
                    state_t = st_ref[bi, h]
                    o = _dot_nt(q_inter, state_t.astype(BF16)) + intra[bi, c, h]
                    kd_h = jnp.where(lane_head == sub, kd, 0.0).astype(BF16)
                    st_ref[bi, h] = state_t * decay + _dot_tn(vh_b, kd_h)
                    ms = jnp.mean(o * o, axis=-1, keepdims=True)
                    on = o * lax.rsqrt(ms + LN_EPS) * ng
                    rh = r_ref[bi, rows, vs]
                    o_ref[bi, rows, vs] = on * (rh * (1.0 / (1.0 + jnp.exp(-rh))))


def _gla(p, wg, bg, ng, batch, seq, ts, nb):
    kern = functools.partial(_gla_kernel, chunks=ts // CHUNK, nb=nb)
    return pl.pallas_call(
        kern,
        grid=(batch // nb, seq // ts),
        in_specs=[
            pl.BlockSpec((nb, ts, GLA_QK), lambda b, s: (b, s, COL_Q // GLA_QK)),
            pl.BlockSpec((nb, ts, GLA_QK), lambda b, s: (b, s, COL_K // GLA_QK)),
            pl.BlockSpec((nb, ts, GLA_WIDTH), lambda b, s: (b, s, COL_V // GLA_WIDTH)),
            pl.BlockSpec((nb, ts, GLA_WIDTH), lambda b, s: (b, s, COL_R // GLA_WIDTH)),
            pl.BlockSpec((nb, ts, A_PAD), lambda b, s: (b, s, COL_A // A_PAD)),
            pl.BlockSpec((A_PAD, GLA_QK), lambda b, s: (0, 0)),
            pl.BlockSpec((1, GLA_QK), lambda b, s: (0, 0)),
            pl.BlockSpec((1, GLA_DV), lambda b, s: (0, 0)),
        ],
        out_specs=pl.BlockSpec((nb, ts, GLA_WIDTH), lambda b, s: (b, s, 0)),
        out_shape=jax.ShapeDtypeStruct((batch, seq, GLA_WIDTH), F32),
        scratch_shapes=[pltpu.VMEM((nb, GLA_HEADS, GLA_DV, LANES), F32)],
        compiler_params=_params(("parallel", "arbitrary")),
        name="gla",
    )(p, p, p, p, p, wg, bg, ng)


def _mix_out_kernel(su_ref, sv_ref, o_ref, h_ref, lg_ref, lb_ref, ws_ref, bs_ref, wo_ref,
                    g1_ref, b1_ref, out_ref, gate_ref, *, windows):
    u = _gelu(su_ref[...])
    v = _layer_norm(_gelu(sv_ref[...]), lg_ref[...], lb_ref[...])
    blk_r = lax.broadcasted_iota(jnp.int32, (SGU_WINDOW, SGU_WINDOW), 0) // CHUNK
    blk_c = lax.broadcasted_iota(jnp.int32, (SGU_WINDOW, SGU_WINDOW), 1) // CHUNK
    keep = blk_c <= blk_r
    for g in range(SGU_GROUPS):
        cs = slice(g * SGU_GROUP_DIM, (g + 1) * SGU_GROUP_DIM)
        w = jnp.where(keep, ws_ref[g], 0.0).astype(BF16)
        bias = bs_ref[g]
        for n in range(windows):
            rs = slice(n * SGU_WINDOW, (n + 1) * SGU_WINDOW)
            sv = _dot(w, v[rs, cs].astype(BF16)) + bias
            gate_ref[rs, cs] = u[rs, cs] * sv
    mix = _dot(o_ref[...].astype(BF16), wo_ref[0:GLA_WIDTH, :])
    mix = mix + _dot(gate_ref[...].astype(BF16), wo_ref[GLA_WIDTH:, :])
    out_ref[...] = _layer_norm(ALPHA * h_ref[...] + mix, g1_ref[...], b1_ref[...])


def _mix_out(p, o, h, lg, lb, ws, bs_full, wo, g1, b1, tm):
    t, d = h.shape
    kern = functools.partial(_mix_out_kernel, windows=tm // SGU_WINDOW)
    const2 = lambda i: (0, 0)
    const3 = lambda i: (0, 0, 0)
    return pl.pallas_call(
        kern,
        grid=(t // tm,),
        in_specs=[
            pl.BlockSpec((tm, SGU_WIDTH), lambda i: (i, COL_SU // SGU_WIDTH)),
            pl.BlockSpec((tm, SGU_WIDTH), lambda i: (i, COL_SV // SGU_WIDTH)),
            pl.BlockSpec((tm, GLA_WIDTH), lambda i: (i, 0)),
            pl.BlockSpec((tm, d), lambda i: (i, 0)),
            pl.BlockSpec((1, SGU_WIDTH), const2),
            pl.BlockSpec((1, SGU_WIDTH), const2),
            pl.BlockSpec((SGU_GROUPS, SGU_WINDOW, SGU_WINDOW), const3),
            pl.BlockSpec((SGU_GROUPS, SGU_WINDOW, SGU_GROUP_DIM), const3),
            pl.BlockSpec((d, d), const2),
            pl.BlockSpec((1, d), const2),
            pl.BlockSpec((1, d), const2),
        ],
        out_specs=pl.BlockSpec((tm, d), lambda i: (i, 0)),
        out_shape=jax.ShapeDtypeStruct((t, d), F32),
        scratch_shapes=[pltpu.VMEM((tm, SGU_WIDTH), F32)],
        compiler_params=_params(("parallel",)),
        name="mix_out",
    )(p, p, o, h, lg, lb, ws, bs_full, wo, g1, b1)


def _sort_network(n):
    pairs = []

    def merge(lo, hi, r):
        step = r * 2
        if step < hi - lo:
            merge(lo, hi, step)
            merge(lo + r, hi, step)
            pairs.extend((i, i + r) for i in range(lo + r, hi - r, step))
        else:
            pairs.append((lo, lo + r))

    def sort(lo, hi):
        if hi - lo >= 1:
            mid = lo + (hi - lo) // 2
            sort(lo, mid)
            sort(mid + 1, hi)
            merge(lo, hi, 1)

    sort(0, n - 1)
    return pairs


def _ce(x, y):
    if y is None:
        return x, None
    if x is None:
        return y, None
    return jnp.maximum(x, y), jnp.minimum(x, y)


def _merge_top(a, b):
    n = len(a)
    v = [_ce(a[k], b[n - 1 - k])[0] for k in range(n)]
    d = n // 2
    while d >= 1:
        for i in range(n):
            if i & d == 0:
                v[i], v[i + d] = _ce(v[i], v[i + d])
        d //= 2
    return v


def _top_sorted(tiles):
    v = list(tiles)
    for i, j in _sort_network(len(v)):
        v[i], v[j] = _ce(v[i], v[j])
    shift = SUBLANES // 2
    while shift >= 1:
        v = _merge_top(v, [pltpu.roll(x, shift, axis=0) for x in v])
        shift //= 2
    return v


def _pad(vals):
    return list(vals) + [None] * (PEER_TOPK - len(vals))


def _fold_kernel(k_ref, wq_ref, o_ref):
    o_ref[...] = lax.dot_general(k_ref[...], wq_ref[...], (((1,), (1,)), ((), ())),
                                 preferred_element_type=F32,
                                 precision=lax.Precision.HIGHEST).astype(BF16)


def _fold_keys(k1, k2, wq):
    d = wq.shape[0]
    keys = jnp.stack([k1, k2])
    return pl.pallas_call(
        _fold_kernel,
        grid=(PEER_HEADS, 2),
        in_specs=[pl.BlockSpec((None, PEER_NKEYS, PEER_DQ_HALF), lambda h, s: (s, 0, 0)),
                  pl.BlockSpec((d, PEER_DQ_HALF), lambda h, s: (0, 2 * h + s))],
        out_specs=pl.BlockSpec((PEER_NKEYS, d), lambda h, s: (2 * h + s, 0)),
        out_shape=jax.ShapeDtypeStruct((PEER_HEADS * 2 * PEER_NKEYS, d), BF16),
        compiler_params=_params(("parallel", "parallel")),
        name="peer_fold",
    )(keys, wq)


def _route_kernel(h_ref, wall_ref, xt_ref, rank2_ref, p2_ref, n1_ref, p1_ref,
                  s_ref, b_ref, a0_ref, pa_ref, pb_ref, cut_ref, extra_ref, invz_ref):
    tm = h_ref.shape[0]
    xt = jnp.transpose(h_ref[...]).astype(BF16)
    xt_ref[...] = xt
    s_ref[...] = _dot(wall_ref[...], xt)
    ntile = PEER_NKEYS // SUBLANES

    for c in range(tm // LANES):
        ls = slice(c * LANES, (c + 1) * LANES)

        def tile(hd, half, k):
            r0 = pl.multiple_of(hd * (2 * PEER_NKEYS) + half * PEER_NKEYS + k * SUBLANES, SUBLANES)
            return s_ref[pl.ds(r0, SUBLANES), ls]

        def sort_body(hd, carry):
            a = _top_sorted([tile(hd, 0, k) for k in range(ntile)])
            b = _top_sorted([tile(hd, 1, k) for k in range(ntile)])
            a0_ref[hd] = a[0]
            for r in range(PEER_TOPK):
                b_ref[hd, r] = b[r]
                pa_ref[r, pl.ds(hd, 1), :] = a[r][0:1, :]
                pb_ref[r, pl.ds(hd, 1), :] = b[r][0:1, :]
            return carry

        lax.fori_loop(0, PEER_HEADS, sort_body, 0)

        a = [pa_ref[r] for r in range(PEER_TOPK)]
        b = [pb_ref[r] for r in range(PEER_TOPK)]
        row = [_pad([a[i] + b[j] for j in range(PEER_TOPK // (i + 1))]) for i in range(SUBLANES)]
        tail = _pad([a[i] + b[0] for i in range(SUBLANES, PEER_TOPK)])
        small = _merge_top(_merge_top(row[4], row[5]), _merge_top(row[6], row[7]))
        mid = _merge_top(_merge_top(row[2], row[3]), small)
        top = _merge_top(row[0], _merge_top(_merge_top(row[1], tail), mid))
        thr = top[PEER_TOPK - 1]
        z = None
        for r in range(PEER_TOPK):
            e = jnp.exp(top[r] - top[0])
            z = e if z is None else z + e
        invz_ref[...] = 1.0 / z
        for j in range(SUBLANES):
            cj = jnp.full_like(thr, float("inf"))
            for i in range(min(PEER_TOPK // (j + 1), SUBLANES)):
                cj = jnp.where(row[i][j] >= thr, a[i], cj)
            if j == 0:
                for i in range(SUBLANES, PEER_TOPK):
                    cj = jnp.where(tail[i - SUBLANES] >= thr, a[i], cj)
            cut_ref[j] = cj
        extra = jnp.zeros_like(thr)
        for j in range(SUBLANES, PEER_TOPK):
            extra = extra + jnp.where(row[0][j] >= thr, 1.0, 0.0)
        extra_ref[...] = extra

        def emit_body(hd, carry):
            head_row = lambda ref, *idx: jnp.broadcast_to(ref[(*idx, pl.ds(hd, 1), slice(None))],
                                                          (SUBLANES, LANES))
            cut = [head_row(cut_ref, j) for j in range(SUBLANES)]
            extra = head_row(extra_ref)
            inv_z = head_row(invz_ref)
            a0 = a0_ref[hd]
            bs = [b_ref[hd, r] for r in range(PEER_TOPK)]
            for kp in range(ntile // 2):
                r2, pp2 = [], []
                for k in (2 * kp, 2 * kp + 1):
                    rows = slice(k * SUBLANES, (k + 1) * SUBLANES)
                    s1k = tile(hd, 0, k)
                    n1 = jnp.zeros_like(s1k)
                    for j in range(SUBLANES):
                        n1 = jnp.where(s1k >= cut[j], float(j + 1), n1)
                    n1_ref[hd, rows, ls] = jnp.where(s1k >= a0, n1 + extra, n1)
                    p1_ref[hd, rows, ls] = jnp.exp(s1k - a0) * inv_z
                    s2k = tile(hd, 1, k)
                    rk = jnp.full_like(s2k, float(PEER_TOPK))
                    for r in range(PEER_TOPK - 1, -1, -1):
                        rk = jnp.where(s2k >= bs[r], float(r), rk)
                    r2.append(rk)
                    pp2.append(jnp.exp(s2k - bs[0]))
                rows2 = slice(kp * 2 * SUBLANES, (kp + 1) * 2 * SUBLANES)
                rank2_ref[hd, rows2, ls] = jnp.concatenate(r2, axis=0).astype(BF16)
                p2_ref[hd, rows2, ls] = jnp.concatenate(pp2, axis=0).astype(BF16)
            return carry

        lax.fori_loop(0, PEER_HEADS, emit_body, 0)


def _route(h, wall, tm):
    t, d = h.shape
    hk = (PEER_HEADS, PEER_NKEYS, t)
    blk = pl.BlockSpec((PEER_HEADS, PEER_NKEYS, tm), lambda i: (0, 0, i))
    vreg = (SUBLANES, LANES)
    return pl.pallas_call(
        _route_kernel,
        grid=(t // tm,),
        in_specs=[
            pl.BlockSpec((tm, d), lambda i: (i, 0)),
            pl.BlockSpec(wall.shape, lambda i: (0, 0)),
        ],
        out_specs=[pl.BlockSpec((d, tm), lambda i: (0, i)), blk, blk, blk, blk],
        out_shape=[
            jax.ShapeDtypeStruct((d, t), BF16),
            jax.ShapeDtypeStruct(hk, BF16),
            jax.ShapeDtypeStruct(hk, BF16),
            jax.ShapeDtypeStruct(hk, F32),
            jax.ShapeDtypeStruct(hk, F32),
        ],
        scratch_shapes=[pltpu.VMEM((wall.shape[0], tm), F32),
                        pltpu.VMEM((PEER_HEADS, PEER_TOPK) + vreg, F32),
                        pltpu.VMEM((PEER_HEADS,) + vreg, F32),
                        pltpu.VMEM((PEER_TOPK,) + vreg, F32),
                        pltpu.VMEM((PEER_TOPK,) + vreg, F32),
                        pltpu.VMEM((SUBLANES,) + vreg, F32),
                        pltpu.VMEM(vreg, F32),
                        pltpu.VMEM(vreg, F32)],
        compiler_params=_params(("parallel",)),
        name="peer_route",
    )(h, wall)


def _peer_kernel(xt_ref, u_hbm, vt_hbm, rank2_ref, p2_ref, n1_ref, p1_ref, h_ref, g_ref, b_ref,
                 out_ref, acc_ref, y_ref, u_buf, vt_buf, u_sem, vt_sem, *, rows):
    e = pl.program_id(1)
    per = pl.num_programs(1)
    step = pl.program_id(0) * per + e
    n_steps = pl.num_programs(0) * per
    te = rows * PEER_NKEYS

    def tile_copies(s, slot):
        e0 = pl.multiple_of((s % per) * te, te)
        return (pltpu.make_async_copy(u_hbm.at[pl.ds(e0, te), :], u_buf.at[slot], u_sem.at[slot]),
                pltpu.make_async_copy(vt_hbm.at[:, pl.ds(e0, te)], vt_buf.at[slot], vt_sem.at[slot]))

    @pl.when(step == 0)
    def _():
        for s in range(PEER_RING - 1):
            for cp in tile_copies(s, s):
                cp.start()

    ahead = step + (PEER_RING - 1)

    @pl.when(ahead < n_steps)
    def _():
        for cp in tile_copies(ahead, ahead % PEER_RING):
            cp.start()

    slot = step % PEER_RING
    for cp in tile_copies(step, slot):
        cp.wait()
    u_ref = u_buf.at[slot]
    vt_ref = vt_buf.at[slot]

    @pl.when(e == 0)
    def _():
        acc_ref[...] = jnp.zeros_like(acc_ref)

    xt = xt_ref[...]
    tm = xt.shape[1]
    pack = 2 * SUBLANES
    slabs = PEER_NKEYS // pack
    zero = jnp.zeros((), BF16)

    def row_tile(ref, hd, l):
        r8 = jnp.broadcast_to(ref[hd, l:l + 1, :], (SUBLANES, tm))
        return jnp.concatenate([r8, r8], axis=0).astype(BF16)[None]

    for l in range(rows):
        es = slice(l * PEER_NKEYS, (l + 1) * PEER_NKEYS)
        act = _gelu(_dot(u_ref[es, :], xt).astype(BF16))
        w = None
        for hd in range(PEER_HEADS):
            keep = rank2_ref[hd].reshape(slabs, pack, tm) < row_tile(n1_ref, hd, l)
            term = jnp.where(keep, p2_ref[hd].reshape(slabs, pack, tm), zero) * row_tile(p1_ref, hd, l)
            w = term if w is None else w + term
        y_ref[es, :] = act * w.reshape(PEER_NKEYS, tm)
    acc_ref[...] += _dot(vt_ref[...], y_ref[...])

    @pl.when(e == pl.num_programs(1) - 1)
    def _():
        ffn = jnp.transpose(acc_ref[...])
        out_ref[...] = _layer_norm(ALPHA * h_ref[...] + ffn, g_ref[...], b_ref[...])


def _peer(xt, u, vt, rank2, p2, n1, p1, h, g, b, tm, te):
    t, d = h.shape
    rows = te // PEER_NKEYS
    kern = functools.partial(_peer_kernel, rows=rows)
    route_blk = pl.BlockSpec((PEER_HEADS, PEER_NKEYS, tm), lambda i, e: (0, 0, i))
    row_blk = pl.BlockSpec((PEER_HEADS, rows, tm), lambda i, e: (0, e, i))
    return pl.pallas_call(
        kern,
        grid=(t // tm, PEER_EXPERTS // te),
        in_specs=[
            pl.BlockSpec((d, tm), lambda i, e: (0, i)),
            pl.BlockSpec(memory_space=pl.ANY),
            pl.BlockSpec(memory_space=pl.ANY),
            route_blk, route_blk, row_blk, row_blk,
            pl.BlockSpec((tm, d), lambda i, e: (i, 0)),
            pl.BlockSpec((1, d), lambda i, e: (0, 0)),
            pl.BlockSpec((1, d), lambda i, e: (0, 0)),
        ],
        out_specs=pl.BlockSpec((tm, d), lambda i, e: (i, 0)),
        out_shape=jax.ShapeDtypeStruct((t, d), F32),
        scratch_shapes=[pltpu.VMEM((d, tm), F32), pltpu.VMEM((te, tm), BF16),
                        pltpu.VMEM((PEER_RING, te, d), BF16), pltpu.VMEM((PEER_RING, d, te), BF16),
                        pltpu.SemaphoreType.DMA((PEER_RING,)), pltpu.SemaphoreType.DMA((PEER_RING,))],
        compiler_params=_params(("arbitrary", "arbitrary")),
        name="peer_dense",
    )(xt, u, vt, rank2, p2, n1, p1, h, g, b)


def _pack_w_in(w):
    q, k, v, r, a, su, sv = jnp.split(
        w, [256, 512, 1024, 1536, 1552, 2064], axis=-1)
    a = jnp.pad(a, ((0, 0), (0, A_PAD - GLA_GATE_RANK)))
    return jnp.concatenate([q, k, v, r, su, sv, a], axis=-1).astype(BF16)


def _forward(x, ln_in_g, ln_in_b, w_in, w_gate_up, b_gate, gla_norm_g, sgu_ln_g, sgu_ln_b,
             sgu_w, sgu_b, w_out, ln1_g, ln1_b, peer_wq, peer_k1, peer_k2, peer_u, peer_v,
             ln2_g, ln2_b, *, tm_proj, ts_gla, nb_gla, tm_mix, tm_route, tm_peer, te_peer):
    batch, seq, d = x.shape
    t = batch * seq
    row = lambda p: p.reshape(1, -1)
    h = None
    for l in range(DEPTH):
        if l == 0:
            h, p = _in_proj_ln(x.reshape(t, d), ln_in_g, ln_in_b, _pack_w_in(w_in[l]), tm_proj)
        else:
            p = _in_proj(h, _pack_w_in(w_in[l]), tm_proj)
        wg = jnp.pad(w_gate_up[l], ((0, A_PAD - GLA_GATE_RANK), (0, 0)))
        o = _gla(p.reshape(batch, seq, P_COLS), wg, row(b_gate[l]), row(gla_norm_g[l]),
                 batch, seq, ts_gla, nb_gla)
        bs_full = jnp.broadcast_to(sgu_b[l][:, :, None],
                                   (SGU_GROUPS, SGU_WINDOW, SGU_GROUP_DIM))
        h = _mix_out(p, o.reshape(t, GLA_WIDTH), h, row(sgu_ln_g[l]), row(sgu_ln_b[l]),
                     sgu_w[l], bs_full, w_out[l].astype(BF16), row(ln1_g[l]), row(ln1_b[l]),
                     tm_mix)
        wall = _fold_keys(peer_k1[l], peer_k2[l], peer_wq[l])
        xt, rank2, p2, n1, p1 = _route(h, wall, tm_route)
        h = _peer(xt, peer_u[l].astype(BF16), jnp.transpose(peer_v[l].astype(BF16)),
                  rank2, p2, n1, p1, h, row(ln2_g[l]), row(ln2_b[l]), tm_peer, te_peer)
    return h.reshape(batch, seq, d)


def kernel(x, ln_in_g, ln_in_b, w_in, w_gate_up, b_gate, gla_norm_g, sgu_ln_g, sgu_ln_b, sgu_w,
           sgu_b, w_out, ln1_g, ln1_b, peer_wq, peer_k1, peer_k2, peer_u, peer_v, ln2_g, ln2_b):
    return _forward(x, ln_in_g, ln_in_b, w_in, w_gate_up, b_gate, gla_norm_g, sgu_ln_g,
                    sgu_ln_b, sgu_w, sgu_b, w_out, ln1_g, ln1_b, peer_wq, peer_k1, peer_k2,
                    peer_u, peer_v, ln2_g, ln2_b,
                    tm_proj=512, ts_gla=256, nb_gla=8, tm_mix=512, tm_route=512, tm_peer=512,
                    te_peer=2048)
```
